```python
import jax, jax.numpy as jnp
from jax import lax
import numpy as np

D_MODEL = 1024
BATCH = 8
SEQ = 4096
DEPTH = 2

CHUNK = 64
N_MIXERS = 2
N_GLA = (DEPTH + N_MIXERS - 1) // N_MIXERS
N_SSD = DEPTH // N_MIXERS
EPS = 1e-6

GLA_HEADS = 4
GLA_DK = D_MODEL // 2
GLA_DV = D_MODEL
GLA_HK = GLA_DK // GLA_HEADS
GLA_HV = GLA_DV // GLA_HEADS
GLA_GATE_RANK = 16
GLA_GATE_TAU = 16.0
GLA_IN = 2 * GLA_DK + 2 * GLA_DV + GLA_GATE_RANK

SSD_INNER = 2 * D_MODEL
SSD_HEADDIM = 64
SSD_HEADS = SSD_INNER // SSD_HEADDIM
SSD_GROUPS = 4
SSD_HPG = SSD_HEADS // SSD_GROUPS
SSD_STATE = 128
SSD_CONV = 4
SSD_CONV_CH = SSD_INNER + 2 * SSD_GROUPS * SSD_STATE
SSD_IN = SSD_INNER + SSD_CONV_CH + SSD_HEADS

N_EXPERTS = 16
N_GROUPS = 4
EXPERTS_PER_GROUP = N_EXPERTS // N_GROUPS
TOPK_GROUP = 1
TOP_K = 2
D_EXPERT = 512

kernel_name = 'hybrid_gla_ssd_moe_adaln'


def rms_normalize(x):
    xf = x.astype(jnp.float32)
    return xf * lax.rsqrt(jnp.mean(xf * xf, axis=-1, keepdims=True) + EPS)


def ada_norm(x, g, shift, scale):
    y = rms_normalize(x) * g.astype(jnp.float32)
    y = y * (1.0 + scale[:, None, :].astype(jnp.float32)) + shift[:, None, :].astype(jnp.float32)
    return y.astype(x.dtype)


def gla_mixer(h, w_in, w_gate2, b_gate2, norm_g, w_out):
    bsz, s, _ = h.shape
    n = s // CHUNK
    proj = h @ w_in
    q, k, v, r, g_lr = jnp.split(proj, [GLA_DK, 2 * GLA_DK, 2 * GLA_DK + GLA_DV, 2 * GLA_DK + 2 * GLA_DV], axis=-1)
    log_a = jax.nn.log_sigmoid((g_lr @ w_gate2 + b_gate2).astype(jnp.float32)) / GLA_GATE_TAU

    def to_chunks(t, d):
        return t.reshape(bsz, n, CHUNK, GLA_HEADS, d).transpose(1, 0, 3, 2, 4).astype(jnp.float32)

    qc = to_chunks(q, GLA_HK) * (GLA_HK ** -0.5)
    kc = to_chunks(k, GLA_HK)
    vc = to_chunks(v, GLA_HV)
    bc = jnp.cumsum(to_chunks(log_a, GLA_HK), axis=3)
    causal = jnp.tril(jnp.ones((CHUNK, CHUNK), dtype=bool))

    def step(state, xs):
        q_c, k_c, v_c, b_c = xs
        diff = b_c[:, :, :, None, :] - b_c[:, :, None, :, :]
        decay = jnp.exp(jnp.where(causal[None, None, :, :, None], diff, -jnp.inf))
        scores = jnp.sum(q_c[:, :, :, None, :] * k_c[:, :, None, :, :] * decay, axis=-1)
        b_last = b_c[:, :, -1, :]
        o = jnp.einsum('bhts,bhsv->bhtv', scores, v_c) + jnp.einsum('bhtd,bhdv->bhtv', q_c * jnp.exp(b_c), state)
        k_end = k_c * jnp.exp(b_last[:, :, None, :] - b_c)
        state = jnp.exp(b_last)[..., None] * state + jnp.einsum('bhsd,bhsv->bhdv', k_end, v_c)
        return state, o

    state0 = jnp.zeros((bsz, GLA_HEADS, GLA_HK, GLA_HV), jnp.float32)
    _, o = lax.scan(step, state0, (qc, kc, vc, bc))
    o = o.transpose(1, 0, 3, 2, 4).reshape(bsz, s, GLA_HEADS, GLA_HV)
    o = rms_normalize(o) * norm_g.astype(jnp.float32)
    o = o.reshape(bsz, s, GLA_DV) * jax.nn.silu(r.astype(jnp.float32))
    return o.astype(h.dtype) @ w_out


def causal_depthwise_conv(u, w):
    return lax.conv_general_dilated(
        u, w[:, None, :].astype(u.dtype), window_strides=(1,), padding=[(w.shape[0] - 1, 0)],
        dimension_numbers=('NWC', 'WIO', 'NWC'), feature_group_count=u.shape[-1])


def ssd_mixer(h, w_in, conv_w, conv_b, dt_bias, a_log, d_skip, norm_g, w_out):
    bsz, s, _ = h.shape
    n = s // CHUNK
    proj = h @ w_in
    z, xbc, dt = jnp.split(proj, [SSD_INNER, SSD_INNER + SSD_CONV_CH], axis=-1)
    xbc = jax.nn.silu(causal_depthwise_conv(xbc, conv_w) + conv_b)
    xs_, b_, c_ = jnp.split(xbc, [SSD_INNER, SSD_INNER + SSD_GROUPS * SSD_STATE], axis=-1)
    dt = jax.nn.softplus(dt.astype(jnp.float32) + dt_bias.astype(jnp.float32))
    a = -jnp.exp(a_log.astype(jnp.float32))
    xh = xs_.reshape(bsz, s, SSD_HEADS, SSD_HEADDIM).astype(jnp.float32)

    def to_chunks(t):
        return jnp.moveaxis(t.reshape(bsz, n, CHUNK, *t.shape[2:]), 1, 0)

    bg = b_.reshape(bsz, s, SSD_GROUPS, SSD_STATE).astype(jnp.float32)
    cg = c_.reshape(bsz, s, SSD_GROUPS, SSD_STATE).astype(jnp.float32)
    causal = jnp.tril(jnp.ones((CHUNK, CHUNK), dtype=bool))

    def step(state, xs):
        x_c, dt_c, b_c, c_c = xs
        acum = jnp.cumsum(dt_c * a, axis=1)
        seg = acum[:, :, None, :] - acum[:, None, :, :]
        lmat = jnp.exp(jnp.where(causal[None, :, :, None], seg, -jnp.inf))
        cb = jnp.repeat(jnp.einsum('btgn,bsgn->btsg', c_c, b_c), SSD_HPG, axis=-1)
        xdt = x_c * dt_c[..., None]
        y_diag = jnp.einsum('btsh,bshp->bthp', cb * lmat, xdt)
        c_h = jnp.repeat(c_c, SSD_HPG, axis=2)
        b_h = jnp.repeat(b_c, SSD_HPG, axis=2)
        y_off = jnp.einsum('bthn,bhpn->bthp', c_h, state) * jnp.exp(acum)[..., None]
        decay_end = jnp.exp(acum[:, -1:, :] - acum)
        state = jnp.exp(acum[:, -1, :])[:, :, None, None] * state + jnp.einsum('bshn,bshp->bhpn', b_h * decay_end[..., None], xdt)
        return state, y_diag + y_off

    state0 = jnp.zeros((bsz, SSD_HEADS, SSD_HEADDIM, SSD_STATE), jnp.float32)
    _, y = lax.scan(step, state0, (to_chunks(xh), to_chunks(dt), to_chunks(bg), to_chunks(cg)))
    y = jnp.moveaxis(y, 0, 1).reshape(bsz, s, SSD_HEADS, SSD_HEADDIM)
    y = y + d_skip.astype(jnp.float32)[:, None] * xh
    y = y.reshape(bsz, s, SSD_INNER) * jax.nn.silu(z.astype(jnp.float32))
    y = rms_normalize(y.reshape(bsz, s, SSD_GROUPS, SSD_INNER // SSD_GROUPS)).reshape(bsz, s, SSD_INNER)
    y = y * norm_g.astype(jnp.float32)
    return y.astype(h.dtype) @ w_out


def moe_ffn(h, router_w, router_b, w_gate, w_up, w_down):
    bsz, s, d = h.shape
    t = h.reshape(-1, d)
    ntok = t.shape[0]
    scores = jax.nn.sigmoid((t @ router_w).astype(jnp.float32))
    biased = scores + router_b.astype(jnp.float32)
    group_score = lax.top_k(biased.reshape(ntok, N_GROUPS, EXPERTS_PER_GROUP), 2)[0].sum(-1)
    _, gidx = lax.top_k(group_score, TOPK_GROUP)
    gmask = jax.nn.one_hot(gidx, N_GROUPS, dtype=jnp.float32).sum(1) > 0
    emask = jnp.repeat(gmask, EXPERTS_PER_GROUP, axis=1)
    _, eidx = lax.top_k(jnp.where(emask, biased, -jnp.inf), TOP_K)
    w_sel = jnp.take_along_axis(scores, eidx, axis=1)
    w_sel = w_sel / jnp.sum(w_sel, axis=-1, keepdims=True)
    gates = jnp.sum(jax.nn.one_hot(eidx, N_EXPERTS, dtype=jnp.float32) * w_sel[..., None], axis=1)
    y = jnp.zeros((ntok, d), jnp.float32)
    for e in range(N_EXPERTS):
        he = jax.nn.silu(t @ w_gate[e]) * (t @ w_up[e])
        y = y + gates[:, e:e + 1] * (he @ w_down[e]).astype(jnp.float32)
    return y.astype(h.dtype).reshape(bsz, s, d)


def setup_inputs(seed: int = 0) -> dict:
    key = jax.random.key(seed)
    ks = jax.random.split(key, 32)
    f32 = jnp.float32

    def nrm(k, shape, scale):
        return jax.random.normal(k, shape, f32) * scale

    dt0 = jnp.exp(jax.random.uniform(ks[14], (N_SSD, SSD_HEADS), f32, np.log(1e-3), np.log(1e-1)))
    return {
        'x': nrm(ks[0], (BATCH, SEQ, D_MODEL), 1.0),
        'c': nrm(ks[1], (BATCH, D_MODEL), 1.0),
        'ada_w': nrm(ks[2], (DEPTH, D_MODEL, 6 * D_MODEL), 0.5 * D_MODEL ** -0.5),
        'ada_b': nrm(ks[3], (DEPTH, 6 * D_MODEL), 0.02),
        'norm_mix': 1.0 + nrm(ks[4], (DEPTH, D_MODEL), 0.02),
        'norm_ffn': 1.0 + nrm(ks[5], (DEPTH, D_MODEL), 0.02),
        'norm_final': 1.0 + nrm(ks[6], (D_MODEL,), 0.02),
        'gla_w_in': nrm(ks[7], (N_GLA, D_MODEL, GLA_IN), D_MODEL ** -0.5),
        'gla_w_gate2': nrm(ks[8], (N_GLA, GLA_GATE_RANK, GLA_DK), GLA_GATE_RANK ** -0.5),
        'gla_b_gate2': nrm(ks[9], (N_GLA, GLA_DK), 0.1),
        'gla_norm': 1.0 + nrm(ks[10], (N_GLA, GLA_HV), 0.02),
        'gla_w_out': nrm(ks[11], (N_GLA, GLA_DV, D_MODEL), GLA_DV ** -0.5),
        'ssd_w_in': nrm(ks[12], (N_SSD, D_MODEL, SSD_IN), D_MODEL ** -0.5),
        'ssd_conv_w': nrm(ks[13], (N_SSD, SSD_CONV, SSD_CONV_CH), SSD_CONV ** -0.5),
        'ssd_conv_b': nrm(ks[15], (N_SSD, SSD_CONV_CH), 0.02),
        'ssd_dt_bias': dt0 + jnp.log(-jnp.expm1(-dt0)),
        'ssd_a_log': jnp.log(jax.random.uniform(ks[16], (N_SSD, SSD_HEADS), f32, 1.0, 16.0)),
        'ssd_d': 1.0 + nrm(ks[17], (N_SSD, SSD_HEADS), 0.1),
        'ssd_norm': 1.0 + nrm(ks[18], (N_SSD, SSD_INNER), 0.02),
        'ssd_w_out': nrm(ks[19], (N_SSD, SSD_INNER, D_MODEL), SSD_INNER ** -0.5),
        'router_w': nrm(ks[20], (D_MODEL, N_EXPERTS), D_MODEL ** -0.5),
        'router_b': nrm(ks[21], (N_EXPERTS,), 0.01),
        'moe_w_gate': nrm(ks[22], (DEPTH, N_EXPERTS, D_MODEL, D_EXPERT), D_MODEL ** -0.5),
        'moe_w_up': nrm(ks[23], (DEPTH, N_EXPERTS, D_MODEL, D_EXPERT), D_MODEL ** -0.5),
        'moe_w_down': nrm(ks[24], (DEPTH, N_EXPERTS, D_EXPERT, D_MODEL), D_EXPERT ** -0.5),
    }


def reference(x, c, ada_w, ada_b, norm_mix, norm_ffn, norm_final,
              gla_w_in, gla_w_gate2, gla_b_gate2, gla_norm, gla_w_out,
              ssd_w_in, ssd_conv_w, ssd_conv_b, ssd_dt_bias, ssd_a_log, ssd_d, ssd_norm, ssd_w_out,
              router_w, router_b, moe_w_gate, moe_w_up, moe_w_down):
    cond = jax.nn.silu(c)
    for i in range(DEPTH):
        mod = cond @ ada_w[i] + ada_b[i]
        sh1, sc1, g1, sh2, sc2, g2 = jnp.split(mod, 6, axis=-1)
        h = ada_norm(x, norm_mix[i], sh1, sc1)
        j = i // N_MIXERS
        if i % N_MIXERS == 0:
            mix = gla_mixer(h, gla_w_in[j], gla_w_gate2[j], gla_b_gate2[j], gla_norm[j], gla_w_out[j])
        else:
            mix = ssd_mixer(h, ssd_w_in[j], ssd_conv_w[j], ssd_conv_b[j], ssd_dt_bias[j], ssd_a_log[j],
                            ssd_d[j], ssd_norm[j], ssd_w_out[j])
        x = x + g1[:, None, :] * mix
        h = ada_norm(x, norm_ffn[i], sh2, sc2)
        x = x + g2[:, None, :] * moe_ffn(h, router_w, router_b, moe_w_gate[i], moe_w_up[i], moe_w_down[i])
    return (rms_normalize(x) * norm_final.astype(jnp.float32)).astype(x.dtype)
```

```python
import functools

import jax
import jax.numpy as jnp
from jax import lax
from jax.experimental import pallas as pl
from jax.experimental.pallas import tpu as pltpu

F32 = jnp.float32
BF16 = jnp.bfloat16

EPS = 1e-6
CHUNK = 64

GLA_HEADS = 4
GLA_GATE_RANK = 16
GLA_GATE_TAU = 16.0

SSD_HEADDIM = 64
SSD_GROUPS = 4
SSD_STATE = 128
SSD_CONV = 4

N_EXPERTS = 16
N_GROUPS = 4
EXPERTS_PER_GROUP = N_EXPERTS // N_GROUPS

LANES = 128
SUBLANES = 8
VMEM_LIMIT = 56 * 1024 * 1024


def _dot(a, b):
    return jnp.dot(a, b, preferred_element_type=F32)


def _dot_nt(a, b):
    return lax.dot_general(a, b, (((1,), (1,)), ((), ())), preferred_element_type=F32)


def _dot_tn(a, b):
    return lax.dot_general(a, b, (((0,), (0,)), ((), ())), preferred_element_type=F32)


def _split2(x):
    hi = x.astype(BF16)
    lo = (x - hi.astype(F32)).astype(BF16)
    return hi, lo


def _dot_hl(a_bf16, x):
    hi, lo = _split2(x)
    return _dot(a_bf16, hi) + _dot(a_bf16, lo)


def _sigmoid(x):
    return 1.0 / (1.0 + jnp.exp(-x))


def _silu(x):
    return x * _sigmoid(x)


def _softplus(x):
    return jnp.maximum(x, 0.0) + jnp.log(1.0 + jnp.exp(-jnp.abs(x)))


def _log_sigmoid(x):
    return jnp.minimum(x, 0.0) - jnp.log(1.0 + jnp.exp(-jnp.abs(x)))


def _rms(x):
    return x * lax.rsqrt(jnp.mean(x * x, axis=-1, keepdims=True) + EPS)


def _ada_norm(x, g, shift, scale):
    return _rms(x) * g * (1.0 + scale) + shift


def _const_spec(shape):
    nd = len(shape)
    return pl.BlockSpec(shape, lambda *_: (0,) * nd, pipeline_mode=pl.Buffered(1))


def _tri(n, dtype):
    r = lax.broadcasted_iota(jnp.int32, (n, n), 0)
    c = lax.broadcasted_iota(jnp.int32, (n, n), 1)
    return (c <= r).astype(dtype)


def _ada_kernel(c_ref, w_ref, b_ref, o_ref):
    cond = _silu(c_ref[...]).astype(BF16)
    o_ref[0] = _dot(cond, w_ref[0].astype(BF16)) + b_ref[0]


def _ada_call(c, ada_w, ada_b):
    depth, d, n = ada_w.shape
    bsz = c.shape[0]
    tn = 1536
    return pl.pallas_call(
        _ada_kernel,
        grid=(depth, n // tn),
        in_specs=[
            pl.BlockSpec((bsz, d), lambda i, j: (0, 0)),
            pl.BlockSpec((1, d, tn), lambda i, j: (i, 0, j)),
            pl.BlockSpec((1, 1, tn), lambda i, j: (i, 0, j)),
        ],
        out_specs=pl.BlockSpec((1, bsz, tn), lambda i, j: (i, 0, j)),
        out_shape=jax.ShapeDtypeStruct((depth, bsz, n), F32),
        compiler_params=pltpu.CompilerParams(
            dimension_semantics=("arbitrary", "arbitrary"), vmem_limit_bytes=VMEM_LIMIT),
        name="ada_mod",
    )(c, ada_w, ada_b.reshape(depth, 1, n))


def _gla_kernel(x_ref, sh_ref, sc_ref, gt_ref, nm_ref, wqkvr_ref, wglr_ref, wg2_ref, bg2_ref,
                hn_ref, wout_ref, o_ref, proj_ref, loga_ref, oacc_ref, state_ref, *, dk, dv):
    hk = dk // GLA_HEADS
    hv = dv // GLA_HEADS
    blk = x_ref.shape[1]

    @pl.when(pl.program_id(1) == 0)
    def _():
        state_ref[...] = jnp.zeros_like(state_ref)

    x = x_ref[0]
    h = _ada_norm(x, nm_ref[...], sh_ref[0], sc_ref[0]).astype(BF16)
    proj_ref[...] = _dot(h, wqkvr_ref[...])
    glr = _dot(h, wglr_ref[...]).astype(BF16)
    z = _dot(glr, wg2_ref[...]) + bg2_ref[...]
    loga_ref[...] = _log_sigmoid(z) * (1.0 / GLA_GATE_TAU)

    tri = _tri(CHUNK, BF16)
    row = lax.broadcasted_iota(jnp.int32, (CHUNK, CHUNK), 0)
    col = lax.broadcasted_iota(jnp.int32, (CHUNK, CHUNK), 1)
    causal = col <= row
    qscale = hk ** -0.5

    def chunk_body(c, carry):
        r0 = pl.multiple_of(c * CHUNK, CHUNK)
        rows = pl.ds(r0, CHUNK)
        b = _dot_hl(tri, loga_ref[rows, :])
        for hd in range(GLA_HEADS):
            bh = b[:, hd * hk:(hd + 1) * hk]
            q = proj_ref[rows, hd * hk:(hd + 1) * hk] * qscale
            k = proj_ref[rows, dk + hd * hk:dk + (hd + 1) * hk]
            v = proj_ref[rows, 2 * dk + hd * hv:2 * dk + (hd + 1) * hv].astype(BF16)
            b_last = bh[CHUNK - 1:CHUNK, :]
            qt = (q * jnp.exp(bh)).astype(BF16)
            kt = (k * jnp.exp(-bh)).astype(BF16)
            k_end = (k * jnp.exp(b_last - bh)).astype(BF16)
            s = jnp.where(causal, _dot_nt(qt, kt), 0.0).astype(BF16)
            st = state_ref[hd]
            o = _dot(s, v) + _dot_nt(qt, st.astype(BF16))
            state_ref[hd] = jnp.exp(b_last) * st + _dot_tn(v, k_end)
            oacc_ref[rows, hd * hv:(hd + 1) * hv] = _rms(o) * hn_ref[...]
        return carry

    lax.fori_loop(0, blk // CHUNK, chunk_body, 0)

    r = proj_ref[:, 2 * dk + dv:2 * dk + 2 * dv]
    og = (oacc_ref[...] * _silu(r)).astype(BF16)
    o_ref[0] = x + gt_ref[0] * _dot(og, wout_ref[...])


def _gla_call(x, sh, sc, gt, norm_g, w_in, w_gate2, b_gate2, head_norm, w_out, *, blk):
    bsz, s, d = x.shape
    dk = w_gate2.shape[1]
    dv = w_out.shape[0]
    nq = 2 * dk + 2 * dv
    w_qkvr = w_in[:, :nq].astype(BF16)
    w_glr = jnp.pad(w_in[:, nq:], ((0, 0), (0, LANES - GLA_GATE_RANK))).astype(BF16)
    w_g2 = jnp.pad(w_gate2, ((0, LANES - GLA_GATE_RANK), (0, 0))).astype(BF16)
    row3 = pl.BlockSpec((1, 1, d), lambda b, l: (b, 0, 0))
    xspec = pl.BlockSpec((1, blk, d), lambda b, l: (b, l, 0))
    return pl.pallas_call(
        functools.partial(_gla_kernel, dk=dk, dv=dv),
        grid=(bsz, s // blk),
        in_specs=[
            xspec, row3, row3, row3,
            _const_spec((1, d)),
            _const_spec((d, nq)),
            _const_spec((d, LANES)),
            _const_spec((LANES, dk)),
            _const_spec((1, dk)),
            _const_spec((1, dv // GLA_HEADS)),
            _const_spec((dv, d)),
        ],
        out_specs=xspec,
        out_shape=jax.ShapeDtypeStruct((bsz, s, d), F32),
        scratch_shapes=[
            pltpu.VMEM((blk, nq), F32),
            pltpu.VMEM((blk, dk), F32),
            pltpu.VMEM((blk, dv), F32),
            pltpu.VMEM((GLA_HEADS, dv // GLA_HEADS, dk // GLA_HEADS), F32),
        ],
        compiler_params=pltpu.CompilerParams(
            dimension_semantics=("arbitrary", "arbitrary"), vmem_limit_bytes=VMEM_LIMIT),
        name="gla_layer",
    )(x, sh, sc, gt, norm_g.reshape(1, d), w_qkvr, w_glr, w_g2, b_gate2.reshape(1, dk),
      head_norm.reshape(1, -1), w_out.astype(BF16))


def _ssd_kernel(x_ref, sh_ref, sc_ref, gt_ref, nm_ref, wz_ref, wxbc_ref, wdt_ref, cw_ref, cb_ref,
                dtb_ref, alog_ref, dfull_ref, gn_ref, wout_ref, expand_ref, pairsel_ref,
                o_ref, pad_ref, xbc_ref, dt_ref, y_ref, state_ref, *, inner, nheads):
    blk = x_ref.shape[1]
    gs = SSD_GROUPS * SSD_STATE
    gw = inner // SSD_GROUPS
    first = pl.program_id(1) == 0

    @pl.when(first)
    def _():
        state_ref[...] = jnp.zeros_like(state_ref)
        pad_ref[0:SUBLANES, :] = jnp.zeros((SUBLANES, pad_ref.shape[1]), F32)

    x = x_ref[0]
    h = _ada_norm(x, nm_ref[...], sh_ref[0], sc_ref[0]).astype(BF16)

    pad_ref[SUBLANES:SUBLANES + blk, :] = _dot(h, wxbc_ref[...])
    conv = cb_ref[...]
    for kk in range(SSD_CONV):
        off = SUBLANES - (SSD_CONV - 1) + kk
        conv = conv + cw_ref[kk:kk + 1, :] * pad_ref[off:off + blk, :]
    xbc_ref[...] = _silu(conv)
    pad_ref[0:SUBLANES, :] = pad_ref[blk:blk + SUBLANES, :]

    dt_ref[...] = _softplus(_dot(h, wdt_ref[...]) + dtb_ref[...])
    a_neg = -jnp.exp(alog_ref[...])
    lane = lax.broadcasted_iota(jnp.int32, (1, LANES), 1)
    a_neg = jnp.where(lane < nheads, a_neg, 0.0)

    tri = _tri(CHUNK, BF16)
    row2 = lax.broadcasted_iota(jnp.int32, (CHUNK, 2 * CHUNK), 0)
    col2 = lax.broadcasted_iota(jnp.int32, (CHUNK, 2 * CHUNK), 1)
    left = col2 < CHUNK
    causal2 = jnp.where(left, col2, col2 - CHUNK) <= row2
    head_lane = lax.broadcasted_iota(jnp.int32, (CHUNK, LANES), 1)
    even = (head_lane % 2) == 0
    npairs = nheads // 2
    hpg = nheads // SSD_GROUPS

    def chunk_body(c, carry):
        r0 = pl.multiple_of(c * CHUNK, CHUNK)
        rows = pl.ds(r0, CHUNK)
        dt = dt_ref[rows, :]
        acum = _dot_hl(tri, dt * a_neg)
        a_last = acum[CHUNK - 1:CHUNK, :]
        w_end = dt * jnp.exp(a_last - acum)
        e_cum = jnp.exp(acum)
        stacked = jnp.concatenate(_split2(w_end) + _split2(e_cum), axis=0)
        ex = _dot(stacked, expand_ref[...])
        w_exp = ex[0:CHUNK] + ex[CHUNK:2 * CHUNK]
        e_exp = ex[2 * CHUNK:3 * CHUNK] + ex[3 * CHUNK:4 * CHUNK]
        def pair_rows(v):
            vv = jnp.concatenate([jnp.where(even, v, 0.0), jnp.where(even, 0.0, v)], axis=0)
            hi, lo = _split2(vv)
            return _dot_nt(pairsel_ref[...], hi) + _dot_nt(pairsel_ref[...], lo)
        acum_t = pair_rows(acum)
        dt_t = pair_rows(dt)

        xs = xbc_ref[rows, 0:inner]
        xw = (xs * w_exp).astype(BF16)
        for g in range(SSD_GROUPS):
            bm = xbc_ref[rows, inner + g * SSD_STATE:inner + (g + 1) * SSD_STATE].astype(BF16)
            cm = xbc_ref[rows, inner + gs + g * SSD_STATE:inner + gs + (g + 1) * SSD_STATE].astype(BF16)
            cb2 = _dot_nt(cm, jnp.concatenate([bm, bm], axis=0))
            cols = slice(g * gw, (g + 1) * gw)
            st = state_ref[g]
            y_off = _dot(cm, st.astype(BF16)) * e_exp[:, cols]
            state_ref[g] = e_exp[CHUNK - 1:CHUNK, cols] * st + _dot_tn(bm, xw[:, cols])
            parts = []
            for pj in range(hpg // 2):
                j = g * (hpg // 2) + pj
                h0 = 2 * j
                a_col = jnp.where(left, acum[:, h0:h0 + 1], acum[:, h0 + 1:h0 + 2])
                seg = a_col - acum_t[j:j + 1, :]
                lmat = jnp.exp(jnp.where(causal2, seg, -jnp.inf))
                wgt = (cb2 * lmat * dt_t[j:j + 1, :]).astype(BF16)
                xp = xs[:, 2 * j * SSD_HEADDIM:(2 * j + 2) * SSD_HEADDIM]
                lane_l = lax.broadcasted_iota(jnp.int32, xp.shape, 1) < SSD_HEADDIM
                rhs = jnp.concatenate([jnp.where(lane_l, xp, 0.0), jnp.where(lane_l, 0.0, xp)], axis=0)
                parts.append(_dot(wgt, rhs.astype(BF16)))
            y_diag = jnp.concatenate(parts, axis=1)
            y_ref[rows, cols] = y_diag + y_off + dfull_ref[:, cols] * xs[:, cols]
        return carry

    lax.fori_loop(0, blk // CHUNK, chunk_body, 0)

    z = _dot(h, wz_ref[...])
    y = y_ref[...] * _silu(z)
    yn = jnp.concatenate([_rms(y[:, g * gw:(g + 1) * gw]) for g in range(SSD_GROUPS)], axis=1)
    yn = (yn * gn_ref[...]).astype(BF16)
    o_ref[0] = x + gt_ref[0] * _dot(yn, wout_ref[...])


def _ssd_call(x, sh, sc, gt, norm_g, w_in, conv_w, conv_b, dt_bias, a_log, d_skip, gnorm, w_out, *, blk):
    bsz, s, d = x.shape
    inner = w_out.shape[0]
    nheads = dt_bias.shape[0]
    gs = SSD_GROUPS * SSD_STATE
    cch = inner + 2 * gs
    assert nheads <= LANES and nheads % (2 * SSD_GROUPS) == 0 and 2 * SSD_HEADDIM == LANES
    w_z = w_in[:, :inner].astype(BF16)
    w_xbc = w_in[:, inner:inner + cch].astype(BF16)
    w_dt = jnp.pad(w_in[:, inner + cch:], ((0, 0), (0, LANES - nheads))).astype(BF16)
    dtb = jnp.pad(dt_bias, (0, LANES - nheads)).reshape(1, LANES)
    alog = jnp.pad(a_log, (0, LANES - nheads)).reshape(1, LANES)
    d_full = jnp.repeat(d_skip, SSD_HEADDIM).reshape(1, inner)
    hh = jnp.arange(LANES)[:, None]
    expand = (hh == (jnp.arange(inner)[None, :] // SSD_HEADDIM)).astype(BF16)
    npair_pad = max(SUBLANES, nheads // 2)
    pairsel = ((jnp.arange(LANES)[None, :] // 2) == jnp.arange(npair_pad)[:, None]).astype(BF16)
    row3 = pl.BlockSpec((1, 1, d), lambda b, l: (b, 0, 0))
    xspec = pl.BlockSpec((1, blk, d), lambda b, l: (b, l, 0))
    return pl.pallas_call(
        functools.partial(_ssd_kernel, inner=inner, nheads=nheads),
        grid=(bsz, s // blk),
        in_specs=[
            xspec, row3, row3, row3,
            _const_spec((1, d)),
            _const_spec((d, inner)),
            _const_spec((d, cch)),
            _const_spec((d, LANES)),
            _const_spec((SSD_CONV, cch)),
            _const_spec((1, cch)),
            _const_spec((1, LANES)),
            _const_spec((1, LANES)),
            _const_spec((1, inner)),
            _const_spec((1, inner)),
            _const_spec((inner, d)),
            _const_spec((LANES, inner)),
            _const_spec((npair_pad, LANES)),
        ],
        out_specs=xspec,
        out_shape=jax.ShapeDtypeStruct((bsz, s, d), F32),
        scratch_shapes=[
            pltpu.VMEM((blk + 2 * SUBLANES, cch), F32),
            pltpu.VMEM((blk, cch), F32),
            pltpu.VMEM((blk, LANES), F32),
            pltpu.VMEM((blk, inner), F32),
            pltpu.VMEM((SSD_GROUPS, SSD_STATE, inner // SSD_GROUPS), F32),
        ],
        compiler_params=pltpu.CompilerParams(
            dimension_semantics=("arbitrary", "arbitrary"), vmem_limit_bytes=VMEM_LIMIT),
        name="ssd_layer",
    )(x, sh, sc, gt, norm_g.reshape(1, d), w_z, w_xbc, w_dt, conv_w, conv_b.reshape(1, cch),
      dtb, alog, d_full, gnorm.reshape(1, inner), w_out.astype(BF16), expand, pairsel)


def _route(logits_t, rb_ref):
    scores = _sigmoid(logits_t)
    sc = [scores[e:e + 1, :] for e in range(N_EXPERTS)]
    bi = [sc[e] + rb_ref[e:e + 1, :] for e in range(N_EXPERTS)]
    gscore = []
    for g in range(N_GROUPS):
        a, b, c, d = bi[4 * g:4 * g + 4]
        hi1, lo1 = jnp.maximum(a, b), jnp.minimum(a, b)
        hi2, lo2 = jnp.maximum(c, d), jnp.minimum(c, d)
        gscore.append(jnp.maximum(hi1, hi2) + jnp.maximum(jnp.minimum(hi1, hi2), jnp.maximum(lo1, lo2)))
    gates = []
    for g in range(N_GROUPS):
        gsel = None
        for o in range(N_GROUPS):
            if o == g:
                continue
            win = (gscore[g] >= gscore[o]) if o > g else (gscore[g] > gscore[o])
            gsel = win if gsel is None else jnp.logical_and(gsel, win)
        sel = []
        for i in range(EXPERTS_PER_GROUP):
            e = 4 * g + i
            rank = jnp.zeros_like(bi[e])
            for j in range(EXPERTS_PER_GROUP):
                if j == i:
                    continue
                o = 4 * g + j
                beats = (bi[o] >= bi[e]) if j < i else (bi[o] > bi[e])
                rank = rank + jnp.where(beats, 1.0, 0.0)
            sel.append(jnp.where(jnp.logical_and(gsel, rank < 2.0), sc[e], 0.0))
        denom = sel[0] + sel[1] + sel[2] + sel[3]
        denom = jnp.where(gsel, denom, 1.0)
        gates.extend([s_ / denom for s_ in sel])
    return jnp.concatenate(gates, axis=0)


def _moe_dense_kernel(x_ref, sh_ref, sc_ref, gt_ref, nm_ref, rwt_ref, rb_ref, wg_ref, wu_ref, wd_ref,
                      fin_ref, o_ref, h_ref, gates_ref, acc_ref, *, final):
    e = pl.program_id(2)

    @pl.when(e == 0)
    def _():
        h = _ada_norm(x_ref[0], nm_ref[...], sh_ref[0], sc_ref[0])
        hb = h.astype(BF16)
        h_ref[...] = hb
        gates_t = _route(_dot_nt(rwt_ref[...], hb), rb_ref)
        gates_ref[...] = gates_t.T
        acc_ref[...] = jnp.zeros_like(acc_ref)

    hb = h_ref[...]
    he = (_silu(_dot(hb, wg_ref[0])) * _dot(hb, wu_ref[0])).astype(BF16)
    ye = _dot(he, wd_ref[0])
    lane = lax.broadcasted_iota(jnp.int32, gates_ref.shape, 1)
    gcol = jnp.sum(jnp.where(lane == e, gates_ref[...], 0.0), axis=1, keepdims=True)
    acc_ref[...] += gcol * ye

    @pl.when(e == N_EXPERTS - 1)
    def _():
        y = x_ref[0] + gt_ref[0] * acc_ref[...]
        if final:
            y = _rms(y) * fin_ref[...]
        o_ref[0] = y


def _moe_dense_call(x, sh, sc, gt, norm_g, router_w, router_b, w_gate, w_up, w_down, norm_final, *, tm, final):
    bsz, s, d = x.shape
    de = w_gate.shape[2]
    row3 = pl.BlockSpec((1, 1, d), lambda b, l, e: (b, 0, 0))
    xspec = pl.BlockSpec((1, tm, d), lambda b, l, e: (b, l, 0))
    rb = jnp.broadcast_to(router_b.reshape(N_EXPERTS, 1), (N_EXPERTS, tm))
    return pl.pallas_call(
        functools.partial(_moe_dense_kernel, final=final),
        grid=(bsz, s // tm, N_EXPERTS),
        in_specs=[
            xspec, row3, row3, row3,
            pl.BlockSpec((1, d), lambda b, l, e: (0, 0)),
            pl.BlockSpec((N_EXPERTS, d), lambda b, l, e: (0, 0)),
            pl.BlockSpec((N_EXPERTS, tm), lambda b, l, e: (0, 0)),
            pl.BlockSpec((1, d, de), lambda b, l, e: (e, 0, 0)),
            pl.BlockSpec((1, d, de), lambda b, l, e: (e, 0, 0)),
            pl.BlockSpec((1, de, d), lambda b, l, e: (e, 0, 0)),
            pl.BlockSpec((1, d), lambda b, l, e: (0, 0)),
        ],
        out_specs=xspec,
        out_shape=jax.ShapeDtypeStruct((bsz, s, d), F32),
        scratch_shapes=[
            pltpu.VMEM((tm, d), BF16),
            pltpu.VMEM((tm, N_EXPERTS), F32),
            pltpu.VMEM((tm, d), F32),
        ],
        compiler_params=pltpu.CompilerParams(
            dimension_semantics=("arbitrary", "arbitrary", "arbitrary"), vmem_limit_bytes=VMEM_LIMIT),
        name="moe_dense",
    )(x, sh, sc, gt, norm_g.reshape(1, d), router_w.T.astype(BF16), rb,
      w_gate.astype(BF16), w_up.astype(BF16), w_down.astype(BF16), norm_final.reshape(1, d))


def kernel(x, c, ada_w, ada_b, norm_mix, norm_ffn, norm_final, gla_w_in, gla_w_gate2, gla_b_gate2, gla_norm,
           gla_w_out, ssd_w_in, ssd_conv_w, ssd_conv_b, ssd_dt_bias, ssd_a_log, ssd_d, ssd_norm, ssd_w_out,
           router_w, router_b, moe_w_gate, moe_w_up, moe_w_down):
    depth = ada_w.shape[0]
    bsz, s, d = x.shape
    mod = _ada_call(c, ada_w, ada_b).reshape(depth, bsz, 6, 1, d)
    n_mixers = 2
    for i in range(depth):
        sh1, sc1, g1, sh2, sc2, g2 = (mod[i, :, t] for t in range(6))
        j = i // n_mixers
        if i % n_mixers == 0:
            x = _gla_call(x, sh1, sc1, g1, norm_mix[i], gla_w_in[j], gla_w_gate2[j], gla_b_gate2[j],
                          gla_norm[j], gla_w_out[j], blk=min(512, s))
        else:
            x = _ssd_call(x, sh1, sc1, g1, norm_mix[i], ssd_w_in[j], ssd_conv_w[j], ssd_conv_b[j],
                          ssd_dt_bias[j], ssd_a_log[j], ssd_d[j], ssd_norm[j], ssd_w_out[j], blk=min(256, s))
        x = _moe_dense_call(x, sh2, sc2, g2, norm_ffn[i], router_w, router_b, moe_w_gate[i], moe_w_up[i],
                            moe_w_down[i], norm_final, tm=min(512, s), final=(i == depth - 1))
    return x
```

```python
import functools

import jax
import jax.numpy as jnp
from jax import lax
from jax.experimental import pallas as pl
from jax.experimental.pallas import tpu as pltpu

F32 = jnp.float32
BF16 = jnp.bfloat16

EPS = 1e-6
CHUNK = 64

GLA_HEADS = 4
GLA_GATE_RANK = 16
GLA_GATE_TAU = 16.0

SSD_HEADDIM = 64
SSD_GROUPS = 4
SSD_STATE = 128
SSD_CONV = 4

N_EXPERTS = 16
N_GROUPS = 4
EXPERTS_PER_GROUP = N_EXPERTS // N_GROUPS

LANES = 128
SUBLANES = 8
VMEM_LIMIT = 56 * 1024 * 1024


def _dot(a, b):
    return jnp.dot(a, b, preferred_element_type=F32)


def _dot_nt(a, b):
    return lax.dot_general(a, b, (((1,), (1,)), ((), ())), preferred_element_type=F32)


def _dot_tn(a, b):
    return lax.dot_general(a, b, (((0,), (0,)), ((), ())), preferred_element_type=F32)


def _split2(x):
    hi = x.astype(BF16)
    lo = (x - hi.astype(F32)).astype(BF16)
    return hi, lo


def _dot_hl(a_bf16, x):
    hi, lo = _split2(x)
    return _dot(a_bf16, hi) + _dot(a_bf16, lo)


def _sigmoid(x):
    return 1.0 / (1.0 + jnp.exp(-x))


def _silu(x):
    return x * _sigmoid(x)


def _softplus(x):
    return jnp.maximum(x, 0.0) + jnp.log(1.0 + jnp.exp(-jnp.abs(x)))


def _log_sigmoid(x):
    return jnp.minimum(x, 0.0) - jnp.log(1.0 + jnp.exp(-jnp.abs(x)))


def _rms(x):
    return x * lax.rsqrt(jnp.mean(x * x, axis=-1, keepdims=True) + EPS)


def _ada_norm(x, g, shift, scale):
    return _rms(x) * g * (1.0 + scale) + shift


def _const_spec(shape):
    nd = len(shape)
    return pl.BlockSpec(shape, lambda *_: (0,) * nd, pipeline_mode=pl.Buffered(1))


def _tri(n, dtype):
    r = lax.broadcasted_iota(jnp.int32, (n, n), 0)
    c = lax.broadcasted_iota(jnp.int32, (n, n), 1)
    return (c <= r).astype(dtype)


def _ada_kernel(c_ref, w_ref, b_ref, o_ref):
    cond = _silu(c_ref[...]).astype(BF16)
    o_ref[0] = _dot(cond, w_ref[0].astype(BF16)) + b_ref[0]


def _ada_call(c, ada_w, ada_b):
    depth, d, n = ada_w.shape
    bsz = c.shape[0]
    tn = 1536
    return pl.pallas_call(
        _ada_kernel,
        grid=(depth, n // tn),
        in_specs=[
            pl.BlockSpec((bsz, d), lambda i, j: (0, 0)),
            pl.BlockSpec((1, d, tn), lambda i, j: (i, 0, j)),
            pl.BlockSpec((1, 1, tn), lambda i, j: (i, 0, j)),
        ],
        out_specs=pl.BlockSpec((1, bsz, tn), lambda i, j: (i, 0, j)),
        out_shape=jax.ShapeDtypeStruct((depth, bsz, n), F32),
        compiler_params=pltpu.CompilerParams(
            dimension_semantics=("arbitrary", "arbitrary"), vmem_limit_bytes=VMEM_LIMIT),
        name="ada_mod",
    )(c, ada_w, ada_b.reshape(depth, 1, n))


def _gla_kernel(x_ref, sh_ref, sc_ref, gt_ref, nm_ref, wqkvr_ref, wglr_ref, wg2_ref, bg2_ref,
                hn_ref, wout_ref, o_ref, proj_ref, loga_ref, oacc_ref, state_ref, *, dk, dv):
    hk = dk // GLA_HEADS
    hv = dv // GLA_HEADS
    blk = x_ref.shape[1]

    @pl.when(pl.program_id(1) == 0)
    def _():
        state_ref[...] = jnp.zeros_like(state_ref)

    x = x_ref[0]
    h = _ada_norm(x, nm_ref[...], sh_ref[0], sc_ref[0]).astype(BF16)
    proj_ref[...] = _dot(h, wqkvr_ref[...])
    glr = _dot(h, wglr_ref[...]).astype(BF16)
    z = _dot(glr, wg2_ref[...]) + bg2_ref[...]
    loga_ref[...] = _log_sigmoid(z) * (1.0 / GLA_GATE_TAU)

    tri = _tri(CHUNK, BF16)
    row = lax.broadcasted_iota(jnp.int32, (CHUNK, CHUNK), 0)
    col = lax.broadcasted_iota(jnp.int32, (CHUNK, CHUNK), 1)
    causal = col <= row
    qscale = hk ** -0.5

    def chunk_body(c, carry):
        r0 = pl.multiple_of(c * CHUNK, CHUNK)
        rows = pl.ds(r0, CHUNK)
        b = _dot_hl(tri, loga_ref[rows, :])
        for hd in range(GLA_HEADS):
            bh = b[:, hd * hk:(hd + 1) * hk]
            q = proj_ref[rows, hd * hk:(hd + 1) * hk] * qscale
            k = proj_ref[rows, dk + hd * hk:dk + (hd + 1) * hk]
            v = proj_ref[rows, 2 * dk + hd * hv:2 * dk + (hd + 1) * hv].astype(BF16)
            b_last = bh[CHUNK - 1:CHUNK, :]
            qt = (q * jnp.exp(bh)).astype(BF16)
            kt = (k * jnp.exp(-bh)).astype(BF16)
            k_end = (k * jnp.exp(b_last - bh)).astype(BF16)
            s = jnp.where(causal, _dot_nt(qt, kt), 0.0).astype(BF16)
            st = state_ref[hd]
            o = _dot(s, v) + _dot_nt(qt, st.astype(BF16))
            state_ref[hd] = jnp.exp(b_last) * st + _dot_tn(v, k_end)
            oacc_ref[rows, hd * hv:(hd + 1) * hv] = _rms(o) * hn_ref[...]
        return carry

    lax.fori_loop(0, blk // CHUNK, chunk_body, 0)

    r = proj_ref[:, 2 * dk + dv:2 * dk + 2 * dv]
    og = (oacc_ref[...] * _silu(r)).astype(BF16)
    o_ref[0] = x + gt_ref[0] * _dot(og, wout_ref[...])


def _gla_call(x, sh, sc, gt, norm_g, w_in, w_gate2, b_gate2, head_norm, w_out, *, blk):
    bsz, s, d = x.shape
    dk = w_gate2.shape[1]
    dv = w_out.shape[0]
    nq = 2 * dk + 2 * dv
    w_qkvr = w_in[:, :nq].astype(BF16)
    w_glr = jnp.pad(w_in[:, nq:], ((0, 0), (0, LANES - GLA_GATE_RANK))).astype(BF16)
    w_g2 = jnp.pad(w_gate2, ((0, LANES - GLA_GATE_RANK), (0, 0))).astype(BF16)
    row3 = pl.BlockSpec((1, 1, d), lambda b, l: (b, 0, 0))
    xspec = pl.BlockSpec((1, blk, d), lambda b, l: (b, l, 0))
    return pl.pallas_call(
        functools.partial(_gla_kernel, dk=dk, dv=dv),
        grid=(bsz, s // blk),
        in_specs=[
            xspec, row3, row3, row3,
            _const_spec((1, d)),
            _const_spec((d, nq)),
            _const_spec((d, LANES)),
            _const_spec((LANES, dk)),
            _const_spec((1, dk)),
            _const_spec((1, dv // GLA_HEADS)),
            _const_spec((dv, d)),
        ],
        out_specs=xspec,
        out_shape=jax.ShapeDtypeStruct((bsz, s, d), F32),
        scratch_shapes=[
            pltpu.VMEM((blk, nq), F32),
            pltpu.VMEM((blk, dk), F32),
            pltpu.VMEM((blk, dv), F32),
            pltpu.VMEM((GLA_HEADS, dv // GLA_HEADS, dk // GLA_HEADS), F32),
        ],
        compiler_params=pltpu.CompilerParams(
            dimension_semantics=("arbitrary", "arbitrary"), vmem_limit_bytes=VMEM_LIMIT),
        name="gla_layer",
    )(x, sh, sc, gt, norm_g.reshape(1, d), w_qkvr, w_glr, w_g2, b_gate2.reshape(1, dk),
      head_norm.reshape(1, -1), w_out.astype(BF16))


def _ssd_kernel(x_ref, sh_ref, sc_ref, gt_ref, nm_ref, wz_ref, wxbc_ref, wdt_ref, cw_ref, cb_ref,
                dtb_ref, alog_ref, dfull_ref, gn_ref, wout_ref, expand_ref, pairsel_ref,
                o_ref, pad_ref, xbc_ref, dt_ref, y_ref, state_ref, *, inner, nheads):
    blk = x_ref.shape[1]
    gs = SSD_GROUPS * SSD_STATE
    gw = inner // SSD_GROUPS
    first = pl.program_id(1) == 0

    @pl.when(first)
    def _():
        state_ref[...] = jnp.zeros_like(state_ref)
        pad_ref[0:SUBLANES, :] = jnp.zeros((SUBLANES, pad_ref.shape[1]), F32)

    x = x_ref[0]
    h = _ada_norm(x, nm_ref[...], sh_ref[0], sc_ref[0]).astype(BF16)

    pad_ref[SUBLANES:SUBLANES + blk, :] = _dot(h, wxbc_ref[...])
    conv = cb_ref[...]
    for kk in range(SSD_CONV):
        off = SUBLANES - (SSD_CONV - 1) + kk
        conv = conv + cw_ref[kk:kk + 1, :] * pad_ref[off:off + blk, :]
    xbc_ref[...] = _silu(conv)
    pad_ref[0:SUBLANES, :] = pad_ref[blk:blk + SUBLANES, :]

    dt_ref[...] = _softplus(_dot(h, wdt_ref[...]) + dtb_ref[...])
    a_neg = -jnp.exp(alog_ref[...])
    lane = lax.broadcasted_iota(jnp.int32, (1, LANES), 1)
    a_neg = jnp.where(lane < nheads, a_neg, 0.0)

    tri = _tri(CHUNK, BF16)
    row2 = lax.broadcasted_iota(jnp.int32, (CHUNK, 2 * CHUNK), 0)
    col2 = lax.broadcasted_iota(jnp.int32, (CHUNK, 2 * CHUNK), 1)
    left = col2 < CHUNK
    causal2 = jnp.where(left, col2, col2 - CHUNK) <= row2
    head_lane = lax.broadcasted_iota(jnp.int32, (CHUNK, LANES), 1)
    even = (head_lane % 2) == 0
    npairs = nheads // 2
    hpg = nheads // SSD_GROUPS

    def chunk_body(c, carry):
        r0 = pl.multiple_of(c * CHUNK, CHUNK)
        rows = pl.ds(r0, CHUNK)
        dt = dt_ref[rows, :]
        acum = _dot_hl(tri, dt * a_neg)
        a_last = acum[CHUNK - 1:CHUNK, :]
        w_end = dt * jnp.exp(a_last - acum)
        e_cum = jnp.exp(acum)
        stacked = jnp.concatenate(_split2(w_end) + _split2(e_cum), axis=0)
        ex = _dot(stacked, expand_ref[...])
        w_exp = ex[0:CHUNK] + ex[CHUNK:2 * CHUNK]
        e_exp = ex[2 * CHUNK:3 * CHUNK] + ex[3 * CHUNK:4 * CHUNK]
        def pair_rows(v):
            vv = jnp.concatenate([jnp.where(even, v, 0.0), jnp.where(even, 0.0, v)], axis=0)
            hi, lo = _split2(vv)
            return _dot_nt(pairsel_ref[...], hi) + _dot_nt(pairsel_ref[...], lo)
        acum_t = pair_rows(acum)
        dt_t = pair_rows(dt)

        xs = xbc_ref[rows, 0:inner]
        xw = (xs * w_exp).astype(BF16)
        for g in range(SSD_GROUPS):
            bm = xbc_ref[rows, inner + g * SSD_STATE:inner + (g + 1) * SSD_STATE].astype(BF16)
            cm = xbc_ref[rows, inner + gs + g * SSD_STATE:inner + gs + (g + 1) * SSD_STATE].astype(BF16)
            cb2 = _dot_nt(cm, jnp.concatenate([bm, bm], axis=0))
            cols = slice(g * gw, (g + 1) * gw)
            st = state_ref[g]
            y_off = _dot(cm, st.astype(BF16)) * e_exp[:, cols]
            state_ref[g] = e_exp[CHUNK - 1:CHUNK, cols] * st + _dot_tn(bm, xw[:, cols])
            parts = []
            for pj in range(hpg // 2):
                j = g * (hpg // 2) + pj
                h0 = 2 * j
                a_col = jnp.where(left, acum[:, h0:h0 + 1], acum[:, h0 + 1:h0 + 2])
                seg = a_col - acum_t[j:j + 1, :]
                lmat = jnp.exp(jnp.where(causal2, seg, -jnp.inf))
                wgt = (cb2 * lmat * dt_t[j:j + 1, :]).astype(BF16)
                xp = xs[:, 2 * j * SSD_HEADDIM:(2 * j + 2) * SSD_HEADDIM]
                lane_l = lax.broadcasted_iota(jnp.int32, xp.shape, 1) < SSD_HEADDIM
                rhs = jnp.concatenate([jnp.where(lane_l, xp, 0.0), jnp.where(lane_l, 0.0, xp)], axis=0)
                parts.append(_dot(wgt, rhs.astype(BF16)))
            y_diag = jnp.concatenate(parts, axis=1)
            y_ref[rows, cols] = y_diag + y_off + dfull_ref[:, cols] * xs[:, cols]
        return carry

    lax.fori_loop(0, blk // CHUNK, chunk_body, 0)

    z = _dot(h, wz_ref[...])
    y = y_ref[...] * _silu(z)
    yn = jnp.concatenate([_rms(y[:, g * gw:(g + 1) * gw]) for g in range(SSD_GROUPS)], axis=1)
    yn = (yn * gn_ref[...]).astype(BF16)
    o_ref[0] = x + gt_ref[0] * _dot(yn, wout_ref[...])


def _ssd_call(x, sh, sc, gt, norm_g, w_in, conv_w, conv_b, dt_bias, a_log, d_skip, gnorm, w_out, *, blk):
    bsz, s, d = x.shape
    inner = w_out.shape[0]
    nheads = dt_bias.shape[0]
    gs = SSD_GROUPS * SSD_STATE
    cch = inner + 2 * gs
    assert nheads <= LANES and nheads % (2 * SSD_GROUPS) == 0 and 2 * SSD_HEADDIM == LANES
    w_z = w_in[:, :inner].astype(BF16)
    w_xbc = w_in[:, inner:inner + cch].astype(BF16)
    w_dt = jnp.pad(w_in[:, inner + cch:], ((0, 0), (0, LANES - nheads))).astype(BF16)
    dtb = jnp.pad(dt_bias, (0, LANES - nheads)).reshape(1, LANES)
    alog = jnp.pad(a_log, (0, LANES - nheads)).reshape(1, LANES)
    d_full = jnp.repeat(d_skip, SSD_HEADDIM).reshape(1, inner)
    hh = jnp.arange(LANES)[:, None]
    expand = (hh == (jnp.arange(inner)[None, :] // SSD_HEADDIM)).astype(BF16)
    npair_pad = max(SUBLANES, nheads // 2)
    pairsel = ((jnp.arange(LANES)[None, :] // 2) == jnp.arange(npair_pad)[:, None]).astype(BF16)
    row3 = pl.BlockSpec((1, 1, d), lambda b, l: (b, 0, 0))
    xspec = pl.BlockSpec((1, blk, d), lambda b, l: (b, l, 0))
    return pl.pallas_call(
        functools.partial(_ssd_kernel, inner=inner, nheads=nheads),
        grid=(bsz, s // blk),
        in_specs=[
            xspec, row3, row3, row3,
            _const_spec((1, d)),
            _const_spec((d, inner)),
            _const_spec((d, cch)),
            _const_spec((d, LANES)),
            _const_spec((SSD_CONV, cch)),
            _const_spec((1, cch)),
            _const_spec((1, LANES)),
            _const_spec((1, LANES)),
            _const_spec((1, inner)),
            _const_spec((1, inner)),
            _const_spec((inner, d)),
            _const_spec((LANES, inner)),
            _const_spec((npair_pad, LANES)),
        ],
        out_specs=xspec,
        out_shape=jax.ShapeDtypeStruct((bsz, s, d), F32),
        scratch_shapes=[
            pltpu.VMEM((blk + 2 * SUBLANES, cch), F32),
            pltpu.VMEM((blk, cch), F32),
            pltpu.VMEM((blk, LANES), F32),
            pltpu.VMEM((blk, inner), F32),
            pltpu.VMEM((SSD_GROUPS, SSD_STATE, inner // SSD_GROUPS), F32),
        ],
        compiler_params=pltpu.CompilerParams(
            dimension_semantics=("arbitrary", "arbitrary"), vmem_limit_bytes=VMEM_LIMIT),
        name="ssd_layer",
    )(x, sh, sc, gt, norm_g.reshape(1, d), w_z, w_xbc, w_dt, conv_w, conv_b.reshape(1, cch),
      dtb, alog, d_full, gnorm.reshape(1, inner), w_out.astype(BF16), expand, pairsel)


def _route(logits_t, rb_ref):
    scores = _sigmoid(logits_t)
    sc = [scores[e:e + 1, :] for e in range(N_EXPERTS)]
    bi = [sc[e] + rb_ref[e:e + 1, :] for e in range(N_EXPERTS)]
    gscore = []
    for g in range(N_GROUPS):
        a, b, c, d = bi[4 * g:4 * g + 4]
        hi1, lo1 = jnp.maximum(a, b), jnp.minimum(a, b)
        hi2, lo2 = jnp.maximum(c, d), jnp.minimum(c, d)
        gscore.append(jnp.maximum(hi1, hi2) + jnp.maximum(jnp.minimum(hi1, hi2), jnp.maximum(lo1, lo2)))
    gates = []
    flags = []
    for g in range(N_GROUPS):
        gsel = None
        for o in range(N_GROUPS):
            if o == g:
                continue
            win = (gscore[g] >= gscore[o]) if o > g else (gscore[g] > gscore[o])
            gsel = win if gsel is None else jnp.logical_and(gsel, win)
        sel = []
        for i in range(EXPERTS_PER_GROUP):
            e = 4 * g + i
            rank = jnp.zeros_like(bi[e])
            for j in range(EXPERTS_PER_GROUP):
                if j == i:
                    continue
                o = 4 * g + j
                beats = (bi[o] >= bi[e]) if j < i else (bi[o] > bi[e])
                rank = rank + jnp.where(beats, 1.0, 0.0)
            chosen = jnp.logical_and(gsel, rank < 2.0)
            flags.append(jnp.where(chosen, 1.0, 0.0))
            sel.append(jnp.where(chosen, sc[e], 0.0))
        denom = sel[0] + sel[1] + sel[2] + sel[3]
        denom = jnp.where(gsel, denom, 1.0)
        gates.extend([s_ / denom for s_ in sel])
    return gates, flags


def _router_kernel(x_ref, sh_ref, sc_ref, nm_ref, rwt_ref, rb_ref, upper_ref,
                   h_ref, ri_ref, wcol_ref, cnt_ref):
    tm = x_ref.shape[0]
    h = _ada_norm(x_ref[...], nm_ref[...], sh_ref[0], sc_ref[0])
    h_ref[...] = h
    gates, flags = _route(_dot_nt(rwt_ref[...], h.astype(BF16)), rb_ref)
    big = float(N_EXPERTS)
    e_lo = functools.reduce(jnp.minimum, [jnp.where(flags[e] > 0.0, float(e), big) for e in range(N_EXPERTS)])
    e_hi = functools.reduce(jnp.maximum, [jnp.where(flags[e] > 0.0, float(e), -1.0) for e in range(N_EXPERTS)])
    flag_mat = jnp.concatenate(flags, axis=0)
    prefix = _dot(flag_mat.astype(BF16), upper_ref[...])

    def pick(eid, rows):
        return functools.reduce(
            jnp.add, [jnp.where(eid == float(e), rows[e], 0.0) for e in range(N_EXPERTS)])

    pre = [prefix[e:e + 1, :] for e in range(N_EXPERTS)]
    zero = jnp.zeros((1, tm), F32)
    ri = jnp.concatenate([e_lo, e_hi, pick(e_lo, pre), pick(e_hi, pre), zero, zero, zero, zero], axis=0)
    ri_ref[0] = ri.astype(jnp.int32)
    wrows = jnp.concatenate([pick(e_lo, gates), pick(e_hi, gates)] + [zero] * (N_EXPERTS - 2), axis=0)
    wcol_ref[...] = wrows.T
    cnt_ref[0] = jnp.broadcast_to(jnp.sum(flag_mat, axis=1, keepdims=True), (N_EXPERTS, LANES))


def _router_call(x2d, sh, sc, norm_g, router_w, router_b, *, tm, tiles_per_batch):
    t, d = x2d.shape
    nt = t // tm
    row3 = pl.BlockSpec((1, 1, d), lambda i: (i // tiles_per_batch, 0, 0))
    rb = jnp.broadcast_to(router_b.reshape(N_EXPERTS, 1), (N_EXPERTS, tm))
    upper = (jnp.arange(tm)[:, None] < jnp.arange(tm)[None, :]).astype(BF16)
    return pl.pallas_call(
        _router_kernel,
        grid=(nt,),
        in_specs=[
            pl.BlockSpec((tm, d), lambda i: (i, 0)), row3, row3,
            _const_spec((1, d)),
            _const_spec((N_EXPERTS, d)),
            _const_spec((N_EXPERTS, tm)),
            _const_spec((tm, tm)),
        ],
        out_specs=[
            pl.BlockSpec((tm, d), lambda i: (i, 0)),
            pl.BlockSpec((1, SUBLANES, tm), lambda i: (i, 0, 0)),
            pl.BlockSpec((tm, N_EXPERTS), lambda i: (i, 0)),
            pl.BlockSpec((1, N_EXPERTS, LANES), lambda i: (i, 0, 0)),
        ],
        out_shape=[
            jax.ShapeDtypeStruct((t, d), F32),
            jax.ShapeDtypeStruct((nt, SUBLANES, tm), jnp.int32),
            jax.ShapeDtypeStruct((t, N_EXPERTS), F32),
            jax.ShapeDtypeStruct((nt, N_EXPERTS, LANES), F32),
        ],
        compiler_params=pltpu.CompilerParams(dimension_semantics=("arbitrary",), vmem_limit_bytes=VMEM_LIMIT),
        name="moe_router",
    )(x2d, sh, sc, norm_g.reshape(1, d), router_w.T.astype(BF16), rb, upper)


def _pos_kernel(ri_ref, toff_ref, pos_ref):
    ri = ri_ref[0]
    off = toff_ref[0]
    eio = lax.broadcasted_iota(jnp.int32, (N_EXPERTS, ri.shape[1]), 0)
    rows = []
    for k in range(2):
        base = jnp.sum(jnp.where(eio == ri[k:k + 1, :], off, 0), axis=0, keepdims=True)
        rows.append(base + ri[2 + k:3 + k, :])
    pos_ref[0] = jnp.concatenate(rows, axis=0)


def _pos_call(ri, toff):
    nt, _, tm = ri.shape
    return pl.pallas_call(
        _pos_kernel,
        grid=(nt,),
        in_specs=[pl.BlockSpec((1, SUBLANES, tm), lambda i: (i, 0, 0)),
                  pl.BlockSpec((1, N_EXPERTS, 1), lambda i: (i, 0, 0))],
        out_specs=pl.BlockSpec((1, 2, tm), lambda i: (i, 0, 0)),
        out_shape=jax.ShapeDtypeStruct((nt, 2, tm), jnp.int32),
        compiler_params=pltpu.CompilerParams(dimension_semantics=("arbitrary",)),
        name="moe_pos",
    )(ri, toff.reshape(nt, N_EXPERTS, 1))


ROW_UNROLL = 8


def _row_copy(src_ref, src_row, dst_ref, dst_row, sem):
    return pltpu.make_async_copy(src_ref.at[pl.ds(src_row, 1)], dst_ref.at[pl.ds(dst_row, 1)], sem)


def _scatter_kernel(pos_ref, h_ref, xs_in_ref, xs_ref, sem):
    del xs_in_ref
    tm = h_ref.shape[0]

    def issue(i, carry):
        for u in range(ROW_UNROLL):
            t = i * ROW_UNROLL + u
            for k in range(2):
                _row_copy(h_ref, t, xs_ref, pos_ref[0, k, t], sem).start()
        return carry

    lax.fori_loop(0, tm // ROW_UNROLL, issue, 0)

    def drain(i, carry):
        for _ in range(2 * ROW_UNROLL):
            _row_copy(h_ref, 0, xs_ref, 0, sem).wait()
        return carry

    lax.fori_loop(0, tm // ROW_UNROLL, drain, 0)


def _scatter_call(pos, h2d, n_rows):
    t, d = h2d.shape
    nt, _, tm = pos.shape
    xs0 = jnp.zeros((n_rows, d), F32)
    return pl.pallas_call(
        _scatter_kernel,
        grid=(nt,),
        in_specs=[
            pl.BlockSpec((1, 2, tm), lambda i: (i, 0, 0), memory_space=pltpu.SMEM),
            pl.BlockSpec((tm, d), lambda i: (i, 0)),
            pl.BlockSpec(memory_space=pl.ANY),
        ],
        out_specs=pl.BlockSpec(memory_space=pl.ANY),
        out_shape=jax.ShapeDtypeStruct((n_rows, d), F32),
        scratch_shapes=[pltpu.SemaphoreType.DMA],
        input_output_aliases={2: 0},
        compiler_params=pltpu.CompilerParams(dimension_semantics=("arbitrary",), has_side_effects=True),
        name="moe_scatter",
    )(pos, h2d, xs0)


def _expert_kernel(te_ref, nact_ref, xs_ref, wg_ref, wu_ref, wd_ref, o_ref):
    j = pl.program_id(0)

    @pl.when(j < nact_ref[0])
    def _():
        xb = xs_ref[...].astype(BF16)
        he = (_silu(_dot(xb, wg_ref[0])) * _dot(xb, wu_ref[0])).astype(BF16)
        o_ref[...] = _dot(he, wd_ref[0])

    @pl.when(j >= nact_ref[0])
    def _():
        o_ref[...] = jnp.zeros_like(o_ref)


def _expert_call(tile_expert, nact, xs, w_gate, w_up, w_down, *, rt):
    n_rows, d = xs.shape
    de = w_gate.shape[2]
    grid_spec = pltpu.PrefetchScalarGridSpec(
        num_scalar_prefetch=2,
        grid=(n_rows // rt,),
        in_specs=[
            pl.BlockSpec((rt, d), lambda j, te, na: (j, 0)),
            pl.BlockSpec((1, d, de), lambda j, te, na: (te[j], 0, 0)),
            pl.BlockSpec((1, d, de), lambda j, te, na: (te[j], 0, 0)),
            pl.BlockSpec((1, de, d), lambda j, te, na: (te[j], 0, 0)),
        ],
        out_specs=pl.BlockSpec((rt, d), lambda j, te, na: (j, 0)),
    )
    return pl.pallas_call(
        _expert_kernel,
        grid_spec=grid_spec,
        out_shape=jax.ShapeDtypeStruct((n_rows, d), F32),
        compiler_params=pltpu.CompilerParams(dimension_semantics=("arbitrary",), vmem_limit_bytes=VMEM_LIMIT),
        name="moe_experts",
    )(tile_expert, nact, xs, w_gate.astype(BF16), w_up.astype(BF16), w_down.astype(BF16))


def _combine_kernel(pos_ref, x_ref, gt_ref, wcol_ref, fin_ref, ys_ref, o_ref, ya_ref, yb_ref, sem, *, final):
    tm = x_ref.shape[0]

    def issue(i, carry):
        for u in range(ROW_UNROLL):
            t = i * ROW_UNROLL + u
            _row_copy(ys_ref, pos_ref[0, 0, t], ya_ref, t, sem).start()
            _row_copy(ys_ref, pos_ref[0, 1, t], yb_ref, t, sem).start()
        return carry

    lax.fori_loop(0, tm // ROW_UNROLL, issue, 0)

    def drain(i, carry):
        for _ in range(2 * ROW_UNROLL):
            _row_copy(ys_ref, 0, ya_ref, 0, sem).wait()
        return carry

    lax.fori_loop(0, tm // ROW_UNROLL, drain, 0)

    w = wcol_ref[...]
    y = x_ref[...] + gt_ref[0] * (w[:, 0:1] * ya_ref[...] + w[:, 1:2] * yb_ref[...])
    if final:
        y = _rms(y) * fin_ref[...]
    o_ref[...] = y


def _combine_call(pos, x2d, gt, wcol, norm_final, ys, *, tiles_per_batch, final):
    t, d = x2d.shape
    nt, _, tm = pos.shape
    return pl.pallas_call(
        functools.partial(_combine_kernel, final=final),
        grid=(nt,),
        in_specs=[
            pl.BlockSpec((1, 2, tm), lambda i: (i, 0, 0), memory_space=pltpu.SMEM),
            pl.BlockSpec((tm, d), lambda i: (i, 0)),
            pl.BlockSpec((1, 1, d), lambda i: (i // tiles_per_batch, 0, 0)),
            pl.BlockSpec((tm, N_EXPERTS), lambda i: (i, 0)),
            _const_spec((1, d)),
            pl.BlockSpec(memory_space=pl.ANY),
        ],
        out_specs=pl.BlockSpec((tm, d), lambda i: (i, 0)),
        out_shape=jax.ShapeDtypeStruct((t, d), F32),
        scratch_shapes=[pltpu.VMEM((tm, d), F32), pltpu.VMEM((tm, d), F32), pltpu.SemaphoreType.DMA],
        compiler_params=pltpu.CompilerParams(dimension_semantics=("arbitrary",), vmem_limit_bytes=VMEM_LIMIT),
        name="moe_combine",
    )(pos, x2d, gt, wcol, norm_final.reshape(1, d), ys)


def _moe_layer(x, sh, sc, gt, norm_g, router_w, router_b, w_gate, w_up, w_down, norm_final, *, tm, rt, final):
    bsz, s, d = x.shape
    t = bsz * s
    x2d = x.reshape(t, d)
    tiles_per_batch = s // tm
    h2d, ri, wcol, cnt = _router_call(x2d, sh, sc, norm_g, router_w, router_b, tm=tm,
                                      tiles_per_batch=tiles_per_batch)
    cnt = cnt[:, :, 0].astype(jnp.int32)
    padded = ((jnp.sum(cnt, axis=0) + rt - 1) // rt) * rt
    ends = jnp.cumsum(padded)
    toff = (ends - padded)[None, :] + jnp.cumsum(cnt, axis=0) - cnt
    n_rows = 2 * t + N_EXPERTS * rt
    starts = jnp.arange(n_rows // rt, dtype=jnp.int32) * rt
    tile_expert = jnp.minimum(jnp.sum(starts[:, None] >= ends[None, :], axis=1), N_EXPERTS - 1).astype(jnp.int32)
    nact = (ends[-1:] // rt).astype(jnp.int32)
    pos = _pos_call(ri, toff.astype(jnp.int32))
    xs = _scatter_call(pos, h2d, n_rows)
    ys = _expert_call(tile_expert, nact, xs, w_gate, w_up, w_down, rt=rt)
    out = _combine_call(pos, x2d, gt, wcol, norm_final, ys, tiles_per_batch=tiles_per_batch, final=final)
    return out.reshape(bsz, s, d)


def kernel(x, c, ada_w, ada_b, norm_mix, norm_ffn, norm_final, gla_w_in, gla_w_gate2, gla_b_gate2, gla_norm,
           gla_w_out, ssd_w_in, ssd_conv_w, ssd_conv_b, ssd_dt_bias, ssd_a_log, ssd_d, ssd_norm, ssd_w_out,
           router_w, router_b, moe_w_gate, moe_w_up, moe_w_down):
    depth = ada_w.shape[0]
    bsz, s, d = x.shape
    mod = _ada_call(c, ada_w, ada_b).reshape(depth, bsz, 6, 1, d)
    n_mixers = 2
    for i in range(depth):
        sh1, sc1, g1, sh2, sc2, g2 = (mod[i, :, t] for t in range(6))
        j = i // n_mixers
        if i % n_mixers == 0:
            x = _gla_call(x, sh1, sc1, g1, norm_mix[i], gla_w_in[j], gla_w_gate2[j], gla_b_gate2[j],
                          gla_norm[j], gla_w_out[j], blk=min(512, s))
        else:
            x = _ssd_call(x, sh1, sc1, g1, norm_mix[i], ssd_w_in[j], ssd_conv_w[j], ssd_conv_b[j],
                          ssd_dt_bias[j], ssd_a_log[j], ssd_d[j], ssd_norm[j], ssd_w_out[j], blk=min(256, s))
        x = _moe_layer(x, sh2, sc2, g2, norm_ffn[i], router_w, router_b, moe_w_gate[i], moe_w_up[i],
                       moe_w_down[i], norm_final, tm=min(512, s), rt=256, final=(i == depth - 1))
    return x
```

```python
import functools

import jax
import jax.numpy as jnp
from jax import lax
from jax.experimental import pallas as pl
from jax.experimental.pallas import tpu as pltpu

F32 = jnp.float32
BF16 = jnp.bfloat16

EPS = 1e-6
CHUNK = 64

GLA_HEADS = 4
GLA_GATE_RANK = 16
GLA_GATE_TAU = 16.0

SSD_HEADDIM = 64
SSD_GROUPS = 4
SSD_STATE = 128
SSD_CONV = 4

N_EXPERTS = 16
N_GROUPS = 4
EXPERTS_PER_GROUP = N_EXPERTS // N_GROUPS

LANES = 128
SUBLANES = 8
VMEM_LIMIT = 56 * 1024 * 1024


def _dot(a, b):
    return jnp.dot(a, b, preferred_element_type=F32)


def _dot_nt(a, b):
    return lax.dot_general(a, b, (((1,), (1,)), ((), ())), preferred_element_type=F32)


def _dot_tn(a, b):
    return lax.dot_general(a, b, (((0,), (0,)), ((), ())), preferred_element_type=F32)


def _split2(x):
    hi = x.astype(BF16)
    lo = (x - hi.astype(F32)).astype(BF16)
    return hi, lo


def _dot_hl(a_bf16, x):
    hi, lo = _split2(x)
    return _dot(a_bf16, hi) + _dot(a_bf16, lo)


def _sigmoid(x):
    return 1.0 / (1.0 + jnp.exp(-x))


def _silu(x):
    return x * _sigmoid(x)


def _softplus(x):
    return jnp.maximum(x, 0.0) + jnp.log(1.0 + jnp.exp(-jnp.abs(x)))


def _log_sigmoid(x):
    return jnp.minimum(x, 0.0) - jnp.log(1.0 + jnp.exp(-jnp.abs(x)))


def _rms(x):
    return x * lax.rsqrt(jnp.mean(x * x, axis=-1, keepdims=True) + EPS)


def _ada_norm(x, g, shift, scale):
    return _rms(x) * g * (1.0 + scale) + shift


def _const_spec(shape):
    nd = len(shape)
    return pl.BlockSpec(shape, lambda *_: (0,) * nd, pipeline_mode=pl.Buffered(1))


def _tri(n, dtype):
    r = lax.broadcasted_iota(jnp.int32, (n, n), 0)
    c = lax.broadcasted_iota(jnp.int32, (n, n), 1)
    return (c <= r).astype(dtype)


def _ada_kernel(c_ref, w_ref, b_ref, o_ref):
    cond = _silu(c_ref[...]).astype(BF16)
    o_ref[0] = _dot(cond, w_ref[0].astype(BF16)) + b_ref[0]


def _ada_call(c, ada_w, ada_b):
    depth, d, n = ada_w.shape
    bsz = c.shape[0]
    tn = 1536
    return pl.pallas_call(
        _ada_kernel,
        grid=(depth, n // tn),
        in_specs=[
            pl.BlockSpec((bsz, d), lambda i, j: (0, 0)),
            pl.BlockSpec((1, d, tn), lambda i, j: (i, 0, j)),
            pl.BlockSpec((1, 1, tn), lambda i, j: (i, 0, j)),
        ],
        out_specs=pl.BlockSpec((1, bsz, tn), lambda i, j: (i, 0, j)),
        out_shape=jax.ShapeDtypeStruct((depth, bsz, n), F32),
        compiler_params=pltpu.CompilerParams(
            dimension_semantics=("arbitrary", "arbitrary"), vmem_limit_bytes=VMEM_LIMIT),
        name="ada_mod",
    )(c, ada_w, ada_b.reshape(depth, 1, n))


def _gla_kernel(x_ref, sh_ref, sc_ref, gt_ref, nm_ref, wqkvr_ref, wglr_ref, wg2_ref, bg2_ref,
                hn_ref, wout_ref, o_ref, proj_ref, loga_ref, oacc_ref, state_ref, *, dk, dv):
    hk = dk // GLA_HEADS
    hv = dv // GLA_HEADS
    blk = x_ref.shape[1]

    @pl.when(pl.program_id(1) == 0)
    def _():
        state_ref[...] = jnp.zeros_like(state_ref)

    x = x_ref[0]
    h = _ada_norm(x, nm_ref[...], sh_ref[0], sc_ref[0]).astype(BF16)
    proj_ref[...] = _dot(h, wqkvr_ref[...])
    glr = _dot(h, wglr_ref[...]).astype(BF16)
    z = _dot(glr, wg2_ref[...]) + bg2_ref[...]
    loga_ref[...] = _log_sigmoid(z) * (1.0 / GLA_GATE_TAU)

    tri = _tri(CHUNK, BF16)
    row = lax.broadcasted_iota(jnp.int32, (CHUNK, CHUNK), 0)
    col = lax.broadcasted_iota(jnp.int32, (CHUNK, CHUNK), 1)
    causal = col <= row
    qscale = hk ** -0.5

    def chunk_body(c, carry):
        r0 = pl.multiple_of(c * CHUNK, CHUNK)
        rows = pl.ds(r0, CHUNK)
        b = _dot_hl(tri, loga_ref[rows, :])
        for hd in range(GLA_HEADS):
            bh = b[:, hd * hk:(hd + 1) * hk]
            q = proj_ref[rows, hd * hk:(hd + 1) * hk] * qscale
            k = proj_ref[rows, dk + hd * hk:dk + (hd + 1) * hk]
            v = proj_ref[rows, 2 * dk + hd * hv:2 * dk + (hd + 1) * hv].astype(BF16)
            b_last = bh[CHUNK - 1:CHUNK, :]
            qt = (q * jnp.exp(bh)).astype(BF16)
            kt = (k * jnp.exp(-bh)).astype(BF16)
            k_end = (k * jnp.exp(b_last - bh)).astype(BF16)
            s = jnp.where(causal, _dot_nt(qt, kt), 0.0).astype(BF16)
            st = state_ref[hd]
            o = _dot(s, v) + _dot_nt(qt, st.astype(BF16))
            state_ref[hd] = jnp.exp(b_last) * st + _dot_tn(v, k_end)
            oacc_ref[rows, hd * hv:(hd + 1) * hv] = _rms(o) * hn_ref[...]
        return carry

    lax.fori_loop(0, blk // CHUNK, chunk_body, 0)

    r = proj_ref[:, 2 * dk + dv:2 * dk + 2 * dv]
    og = (oacc_ref[...] * _silu(r)).astype(BF16)
    o_ref[0] = x + gt_ref[0] * _dot(og, wout_ref[...])


def _gla_call(x, sh, sc, gt, norm_g, w_in, w_gate2, b_gate2, head_norm, w_out, *, blk):
    bsz, s, d = x.shape
    dk = w_gate2.shape[1]
    dv = w_out.shape[0]
    nq = 2 * dk + 2 * dv
    w_qkvr = w_in[:, :nq].astype(BF16)
    w_glr = jnp.pad(w_in[:, nq:], ((0, 0), (0, LANES - GLA_GATE_RANK))).astype(BF16)
    w_g2 = jnp.pad(w_gate2, ((0, LANES - GLA_GATE_RANK), (0, 0))).astype(BF16)
    row3 = pl.BlockSpec((1, 1, d), lambda b, l: (b, 0, 0))
    xspec = pl.BlockSpec((1, blk, d), lambda b, l: (b, l, 0))
    return pl.pallas_call(
        functools.partial(_gla_kernel, dk=dk, dv=dv),
        grid=(bsz, s // blk),
        in_specs=[
            xspec, row3, row3, row3,
            _const_spec((1, d)),
            _const_spec((d, nq)),
            _const_spec((d, LANES)),
            _const_spec((LANES, dk)),
            _const_spec((1, dk)),
            _const_spec((1, dv // GLA_HEADS)),
            _const_spec((dv, d)),
        ],
        out_specs=xspec,
        out_shape=jax.ShapeDtypeStruct((bsz, s, d), F32),
        scratch_shapes=[
            pltpu.VMEM((blk, nq), F32),
            pltpu.VMEM((blk, dk), F32),
            pltpu.VMEM((blk, dv), F32),
            pltpu.VMEM((GLA_HEADS, dv // GLA_HEADS, dk // GLA_HEADS), F32),
        ],
        compiler_params=pltpu.CompilerParams(
            dimension_semantics=("arbitrary", "arbitrary"), vmem_limit_bytes=VMEM_LIMIT),
        name="gla_layer",
    )(x, sh, sc, gt, norm_g.reshape(1, d), w_qkvr, w_glr, w_g2, b_gate2.reshape(1, dk),
      head_norm.reshape(1, -1), w_out.astype(BF16))


def _ssd_kernel(x_ref, sh_ref, sc_ref, gt_ref, nm_ref, wz_ref, wxbc_ref, wdt_ref, cw_ref, cb_ref,
                dtb_ref, alog_ref, dfull_ref, gn_ref, wout_ref, pairsel_ref,
                o_ref, pad_ref, xbc_ref, dt_ref, y_ref, state_ref, *, inner, nheads):
    blk = x_ref.shape[1]
    gs = SSD_GROUPS * SSD_STATE
    gw = inner // SSD_GROUPS
    first = pl.program_id(1) == 0

    @pl.when(first)
    def _():
        state_ref[...] = jnp.zeros_like(state_ref)
        pad_ref[0:SUBLANES, :] = jnp.zeros((SUBLANES, pad_ref.shape[1]), F32)

    x = x_ref[0]
    h = _ada_norm(x, nm_ref[...], sh_ref[0], sc_ref[0]).astype(BF16)

    pad_ref[SUBLANES:SUBLANES + blk, :] = _dot(h, wxbc_ref[...])
    conv = cb_ref[...]
    for kk in range(SSD_CONV):
        off = SUBLANES - (SSD_CONV - 1) + kk
        conv = conv + cw_ref[kk:kk + 1, :] * pad_ref[off:off + blk, :]
    xbc_ref[...] = _silu(conv)
    pad_ref[0:SUBLANES, :] = pad_ref[blk:blk + SUBLANES, :]

    dt_ref[...] = _softplus(_dot(h, wdt_ref[...]) + dtb_ref[...])
    a_neg = -jnp.exp(alog_ref[...])
    lane = lax.broadcasted_iota(jnp.int32, (1, LANES), 1)
    a_neg = jnp.where(lane < nheads, a_neg, 0.0)

    tri = _tri(CHUNK, BF16)
    row2 = lax.broadcasted_iota(jnp.int32, (CHUNK, 2 * CHUNK), 0)
    col2 = lax.broadcasted_iota(jnp.int32, (CHUNK, 2 * CHUNK), 1)
    left = col2 < CHUNK
    causal2 = jnp.where(left, col2, col2 - CHUNK) <= row2
    head_lane = lax.broadcasted_iota(jnp.int32, (CHUNK, LANES), 1)
    even = (head_lane % 2) == 0
    npairs = nheads // 2
    hpg = nheads // SSD_GROUPS

    def chunk_body(c, carry):
        r0 = pl.multiple_of(c * CHUNK, CHUNK)
        rows = pl.ds(r0, CHUNK)
        dt = dt_ref[rows, :]
        acum = _dot_hl(tri, dt * a_neg)
        a_last = acum[CHUNK - 1:CHUNK, :]
        w_end = dt * jnp.exp(a_last - acum)
        def pair_rows(v):
            vv = jnp.concatenate([jnp.where(even, v, 0.0), jnp.where(even, 0.0, v)], axis=0)
            hi, lo = _split2(vv)
            return _dot_nt(pairsel_ref[...], hi) + _dot_nt(pairsel_ref[...], lo)
        acum_t = pair_rows(acum)
        dt_t = pair_rows(dt)

        def pair_cols(v, h0):
            return jnp.where(left, v[:, h0:h0 + 1], v[:, h0 + 1:h0 + 2])

        pw = 2 * SSD_HEADDIM
        for g in range(SSD_GROUPS):
            bm = xbc_ref[rows, inner + g * SSD_STATE:inner + (g + 1) * SSD_STATE].astype(BF16)
            cm = xbc_ref[rows, inner + gs + g * SSD_STATE:inner + gs + (g + 1) * SSD_STATE].astype(BF16)
            cb2 = _dot_nt(cm, jnp.concatenate([bm, bm], axis=0))
            st = state_ref[g]
            c_st = _dot(cm, st.astype(BF16))
            xw_parts, decay_parts = [], []
            for pj in range(hpg // 2):
                j = g * (hpg // 2) + pj
                h0 = 2 * j
                a_col = pair_cols(acum, h0)
                e_col = jnp.exp(a_col)
                xp = xbc_ref[rows, j * pw:(j + 1) * pw]
                xw_parts.append((xp * pair_cols(w_end, h0)).astype(BF16))
                decay_parts.append(e_col[CHUNK - 1:CHUNK, :])
                seg = a_col - acum_t[j:j + 1, :]
                lmat = jnp.exp(jnp.where(causal2, seg, -jnp.inf))
                wgt = (cb2 * lmat * dt_t[j:j + 1, :]).astype(BF16)
                rhs = jnp.concatenate([jnp.where(left, xp, 0.0), jnp.where(left, 0.0, xp)], axis=0)
                y_diag = _dot(wgt, rhs.astype(BF16))
                y_off = e_col * c_st[:, pj * pw:(pj + 1) * pw]
                y_ref[rows, j * pw:(j + 1) * pw] = y_diag + y_off + dfull_ref[:, j * pw:(j + 1) * pw] * xp
            state_ref[g] = (jnp.concatenate(decay_parts, axis=1) * st
                            + _dot_tn(bm, jnp.concatenate(xw_parts, axis=1)))
        return carry

    lax.fori_loop(0, blk // CHUNK, chunk_body, 0)

    z = _dot(h, wz_ref[...])
    y = y_ref[...] * _silu(z)
    yn = jnp.concatenate([_rms(y[:, g * gw:(g + 1) * gw]) for g in range(SSD_GROUPS)], axis=1)
    yn = (yn * gn_ref[...]).astype(BF16)
    o_ref[0] = x + gt_ref[0] * _dot(yn, wout_ref[...])


def _ssd_call(x, sh, sc, gt, norm_g, w_in, conv_w, conv_b, dt_bias, a_log, d_skip, gnorm, w_out, *, blk):
    bsz, s, d = x.shape
    inner = w_out.shape[0]
    nheads = dt_bias.shape[0]
    gs = SSD_GROUPS * SSD_STATE
    cch = inner + 2 * gs
    assert nheads <= LANES and nheads % (2 * SSD_GROUPS) == 0 and 2 * SSD_HEADDIM == LANES
    w_z = w_in[:, :inner].astype(BF16)
    w_xbc = w_in[:, inner:inner + cch].astype(BF16)
    w_dt = jnp.pad(w_in[:, inner + cch:], ((0, 0), (0, LANES - nheads))).astype(BF16)
    dtb = jnp.pad(dt_bias, (0, LANES - nheads)).reshape(1, LANES)
    alog = jnp.pad(a_log, (0, LANES - nheads)).reshape(1, LANES)
    d_full = jnp.repeat(d_skip, SSD_HEADDIM).reshape(1, inner)
    npair_pad = max(SUBLANES, nheads // 2)
    pairsel = ((jnp.arange(LANES)[None, :] // 2) == jnp.arange(npair_pad)[:, None]).astype(BF16)
    row3 = pl.BlockSpec((1, 1, d), lambda b, l: (b, 0, 0))
    xspec = pl.BlockSpec((1, blk, d), lambda b, l: (b, l, 0))
    return pl.pallas_call(
        functools.partial(_ssd_kernel, inner=inner, nheads=nheads),
        grid=(bsz, s // blk),
        in_specs=[
            xspec, row3, row3, row3,
            _const_spec((1, d)),
            _const_spec((d, inner)),
            _const_spec((d, cch)),
            _const_spec((d, LANES)),
            _const_spec((SSD_CONV, cch)),
            _const_spec((1, cch)),
            _const_spec((1, LANES)),
            _const_spec((1, LANES)),
            _const_spec((1, inner)),
            _const_spec((1, inner)),
            _const_spec((inner, d)),
            _const_spec((npair_pad, LANES)),
        ],
        out_specs=xspec,
        out_shape=jax.ShapeDtypeStruct((bsz, s, d), F32),
        scratch_shapes=[
            pltpu.VMEM((blk + 2 * SUBLANES, cch), F32),
            pltpu.VMEM((blk, cch), F32),
            pltpu.VMEM((blk, LANES), F32),
            pltpu.VMEM((blk, inner), F32),
            pltpu.VMEM((SSD_GROUPS, SSD_STATE, inner // SSD_GROUPS), F32),
        ],
        compiler_params=pltpu.CompilerParams(
            dimension_semantics=("arbitrary", "arbitrary"), vmem_limit_bytes=VMEM_LIMIT),
        name="ssd_layer",
    )(x, sh, sc, gt, norm_g.reshape(1, d), w_z, w_xbc, w_dt, conv_w, conv_b.reshape(1, cch),
      dtb, alog, d_full, gnorm.reshape(1, inner), w_out.astype(BF16), pairsel)


def _route(logits_t, rb_ref):
    scores = _sigmoid(logits_t)
    sc = [scores[e:e + 1, :] for e in range(N_EXPERTS)]
    bi = [sc[e] + rb_ref[e:e + 1, :] for e in range(N_EXPERTS)]
    gscore = []
    for g in range(N_GROUPS):
        a, b, c, d = bi[4 * g:4 * g + 4]
        hi1, lo1 = jnp.maximum(a, b), jnp.minimum(a, b)
        hi2, lo2 = jnp.maximum(c, d), jnp.minimum(c, d)
        gscore.append(jnp.maximum(hi1, hi2) + jnp.maximum(jnp.minimum(hi1, hi2), jnp.maximum(lo1, lo2)))
    gates = []
    flags = []
    for g in range(N_GROUPS):
        gsel = None
        for o in range(N_GROUPS):
            if o == g:
                continue
            win = (gscore[g] >= gscore[o]) if o > g else (gscore[g] > gscore[o])
            gsel = win if gsel is None else jnp.logical_and(gsel, win)
        sel = []
        for i in range(EXPERTS_PER_GROUP):
            e = 4 * g + i
            rank = jnp.zeros_like(bi[e])
            for j in range(EXPERTS_PER_GROUP):
                if j == i:
                    continue
                o = 4 * g + j
                beats = (bi[o] >= bi[e]) if j < i else (bi[o] > bi[e])
                rank = rank + jnp.where(beats, 1.0, 0.0)
            chosen = jnp.logical_and(gsel, rank < 2.0)
            flags.append(jnp.where(chosen, 1.0, 0.0))
            sel.append(jnp.where(chosen, sc[e], 0.0))
        denom = sel[0] + sel[1] + sel[2] + sel[3]
        denom = jnp.where(gsel, denom, 1.0)
        gates.extend([s_ / denom for s_ in sel])
    return gates, flags


def _router_kernel(x_ref, sh_ref, sc_ref, nm_ref, rwt_ref, rb_ref, upper_ref,
                   h_ref, slot_ref, col_ref, cnt_ref):
    tm = x_ref.shape[0]
    hb = _ada_norm(x_ref[...], nm_ref[...], sh_ref[0], sc_ref[0]).astype(BF16)
    h_ref[...] = hb
    gates, flags = _route(_dot_nt(rwt_ref[...], hb), rb_ref)
    big = float(N_EXPERTS)
    e_lo = functools.reduce(jnp.minimum, [jnp.where(flags[e] > 0.0, float(e), big) for e in range(N_EXPERTS)])
    e_hi = functools.reduce(jnp.maximum, [jnp.where(flags[e] > 0.0, float(e), -1.0) for e in range(N_EXPERTS)])
    flag_mat = jnp.concatenate(flags, axis=0)
    prefix = _dot(flag_mat.astype(BF16), upper_ref[...])
    counts = jnp.sum(flag_mat, axis=1, keepdims=True)
    slot_of = []
    before = jnp.zeros((1, 1), F32)
    for e in range(N_EXPERTS):
        slot_of.append(prefix[e:e + 1, :] + before)
        before = before + jnp.floor((counts[e:e + 1, :] + (SUBLANES - 1)) * (1.0 / SUBLANES)) * SUBLANES

    def pick(eid, rows):
        return functools.reduce(
            jnp.add, [jnp.where(eid == float(e), rows[e], 0.0) for e in range(N_EXPERTS)])

    zero = jnp.zeros((1, tm), F32)
    s_lo, s_hi = pick(e_lo, slot_of), pick(e_hi, slot_of)
    slot_ref[0] = jnp.concatenate([s_lo, s_hi] + [zero] * (SUBLANES - 2), axis=0).astype(jnp.int32)
    rows = jnp.concatenate([s_lo, s_hi, pick(e_lo, gates), pick(e_hi, gates)] + [zero] * (N_EXPERTS - 4), axis=0)
    col_ref[...] = rows.T
    cnt_ref[0] = jnp.broadcast_to(counts, (N_EXPERTS, LANES))


def _router_call(x2d, sh, sc, norm_g, router_w, router_b, *, tm, tiles_per_batch):
    t, d = x2d.shape
    nt = t // tm
    row3 = pl.BlockSpec((1, 1, d), lambda i: (i // tiles_per_batch, 0, 0))
    rb = jnp.broadcast_to(router_b.reshape(N_EXPERTS, 1), (N_EXPERTS, tm))
    upper = (jnp.arange(tm)[:, None] < jnp.arange(tm)[None, :]).astype(BF16)
    return pl.pallas_call(
        _router_kernel,
        grid=(nt,),
        in_specs=[
            pl.BlockSpec((tm, d), lambda i: (i, 0)), row3, row3,
            _const_spec((1, d)),
            _const_spec((N_EXPERTS, d)),
            _const_spec((N_EXPERTS, tm)),
            _const_spec((tm, tm)),
        ],
        out_specs=[
            pl.BlockSpec((tm, d), lambda i: (i, 0)),
            pl.BlockSpec((1, SUBLANES, tm), lambda i: (i, 0, 0)),
            pl.BlockSpec((tm, N_EXPERTS), lambda i: (i, 0)),
            pl.BlockSpec((1, N_EXPERTS, LANES), lambda i: (i, 0, 0)),
        ],
        out_shape=[
            jax.ShapeDtypeStruct((t, d), BF16),
            jax.ShapeDtypeStruct((nt, SUBLANES, tm), jnp.int32),
            jax.ShapeDtypeStruct((t, N_EXPERTS), F32),
            jax.ShapeDtypeStruct((nt, N_EXPERTS, LANES), F32),
        ],
        compiler_params=pltpu.CompilerParams(dimension_semantics=("arbitrary",), vmem_limit_bytes=VMEM_LIMIT),
        name="moe_router",
    )(x2d, sh, sc, norm_g.reshape(1, d), router_w.T.astype(BF16), rb, upper)


def _pack_pair(a, b):
    ua = lax.bitcast_convert_type(a, jnp.uint32)
    ub = lax.bitcast_convert_type(b, jnp.uint32)
    return (ua >> 16) | (ub & jnp.uint32(0xFFFF0000))


def _unpack_pair(w):
    a = lax.bitcast_convert_type(w << 16, F32)
    b = lax.bitcast_convert_type(w & jnp.uint32(0xFFFF0000), F32)
    return a.astype(BF16), b.astype(BF16)


def _bf16_exact(x):
    return x.astype(BF16).astype(F32)


def _slot_rows(tm):
    return 2 * tm + N_EXPERTS * SUBLANES


def _run_copies(n_rows, max_rows, make_copy, op):
    for b in reversed(range(SUBLANES.bit_length() - 1, max_rows.bit_length())):
        size = 1 << b
        offset = pl.multiple_of((n_rows >> (b + 1)) << (b + 1), SUBLANES)

        @pl.when((n_rows & size) != 0)
        def _(size=size, offset=offset):
            op(make_copy(offset, size))


def _tile_runs(tile, len_ref, lbase_ref, toff_ref, local_ref, global_ref, sem, max_rows, to_global, op):
    for e in range(N_EXPERTS):
        idx = tile * N_EXPERTS + e
        n = len_ref[idx]
        lo = lbase_ref[idx]
        go = toff_ref[idx]

        def make_copy(offset, size, lo=lo, go=go):
            loc = local_ref.at[pl.ds(pl.multiple_of(lo + offset, SUBLANES), size)]
            glb = global_ref.at[pl.ds(pl.multiple_of(go + offset, SUBLANES), size)]
            return pltpu.make_async_copy(loc, glb, sem) if to_global else pltpu.make_async_copy(glb, loc, sem)

        _run_copies(n, max_rows, make_copy, op)


def _start(copy):
    copy.start()


def _wait(copy):
    copy.wait()


def _dispatch_kernel(cnt_ref, lbase_ref, toff_ref, padstart_ref, padlen_ref, nact_ref, slot_ref, h_ref, xs_ref,
                     buf_ref, zero_ref, sem, zsem, *, rt):
    i = pl.program_id(0)
    last = pl.num_programs(0) - 1
    cur = i % 2
    tm = h_ref.shape[0]
    half = h_ref.shape[1] // 2

    nslot = _slot_rows(tm)

    def runs(tile, which, op):
        _tile_runs(tile, cnt_ref, lbase_ref, toff_ref, buf_ref.at[which], xs_ref, sem.at[which], nslot, True, op)

    @pl.when(i > 0)
    def _():
        runs(i - 1, 1 - cur, _wait)

    sl = slot_ref[0]
    srow = lax.broadcasted_iota(jnp.int32, (nslot, tm), 0)
    perm = jnp.where(srow == sl[0:1, :], 1.0, jnp.where(srow == sl[1:2, :], 1.0, 0.0)).astype(BF16)
    rows = _dot(perm, h_ref[...])
    buf_ref[cur] = _pack_pair(rows[:, :half], rows[:, half:])
    runs(i, cur, _start)

    @pl.when(i == last)
    def _():
        runs(i, cur, _wait)
        zero_ref[...] = jnp.zeros_like(zero_ref)
        for op in (_start, _wait):
            for e in range(N_EXPERTS):
                def make_copy(offset, size, e=e):
                    dst = xs_ref.at[pl.ds(pl.multiple_of(padstart_ref[e] + offset, SUBLANES), size)]
                    return pltpu.make_async_copy(zero_ref.at[pl.ds(0, size)], dst, zsem)
                _run_copies(padlen_ref[e], rt - 1, make_copy, op)

        def zero_tile(j, carry):
            copy = pltpu.make_async_copy(zero_ref, xs_ref.at[pl.ds(pl.multiple_of(j * rt, rt), rt)], zsem)
            copy.start()
            copy.wait()
            return carry

        lax.fori_loop(nact_ref[0], xs_ref.shape[0] // rt, zero_tile, 0)


def _dispatch_call(seg, slot_rows, h2d, n_rows, *, rt):
    t, d = h2d.shape
    nt, _, tm = slot_rows.shape
    grid_spec = pltpu.PrefetchScalarGridSpec(
        num_scalar_prefetch=6,
        grid=(nt,),
        in_specs=[
            pl.BlockSpec((1, SUBLANES, tm), lambda i, *_: (i, 0, 0)),
            pl.BlockSpec((tm, d), lambda i, *_: (i, 0)),
        ],
        out_specs=pl.BlockSpec(memory_space=pl.ANY),
        scratch_shapes=[
            pltpu.VMEM((2, _slot_rows(tm), d // 2), jnp.uint32),
            pltpu.VMEM((rt, d // 2), jnp.uint32),
            pltpu.SemaphoreType.DMA((2,)),
            pltpu.SemaphoreType.DMA,
        ],
    )
    return pl.pallas_call(
        functools.partial(_dispatch_kernel, rt=rt),
        grid_spec=grid_spec,
        out_shape=jax.ShapeDtypeStruct((n_rows, d // 2), jnp.uint32),
        compiler_params=pltpu.CompilerParams(dimension_semantics=("arbitrary",), vmem_limit_bytes=VMEM_LIMIT),
        name="moe_dispatch",
    )(seg["cnt"], seg["lbase"], seg["toff"], seg["padstart"], seg["padlen"], seg["nact"], slot_rows, h2d)


def _expert_kernel(te_ref, nact_ref, xs_ref, wg_ref, wu_ref, wd_ref, o_ref, wgb_ref, wub_ref, wdb_ref):
    j = pl.program_id(0)
    active = j < nact_ref[0]
    half = xs_ref.shape[1]

    @pl.when(jnp.logical_and(active, jnp.logical_or(j == 0, te_ref[j] != te_ref[jnp.maximum(j - 1, 0)])))
    def _():
        wgb_ref[...] = wg_ref[0].astype(BF16)
        wub_ref[...] = wu_ref[0].astype(BF16)
        wdb_ref[...] = wd_ref[0].astype(BF16)

    @pl.when(active)
    def _():
        xa, xb = _unpack_pair(xs_ref[...])
        gate = _dot(xa, wgb_ref[0:half, :]) + _dot(xb, wgb_ref[half:, :])
        up = _dot(xa, wub_ref[0:half, :]) + _dot(xb, wub_ref[half:, :])
        he = (_silu(gate) * up).astype(BF16)
        y = _dot(he, wdb_ref[...])
        o_ref[...] = _pack_pair(_bf16_exact(y[:, :half]), _bf16_exact(y[:, half:]))

    @pl.when(jnp.logical_not(active))
    def _():
        o_ref[...] = jnp.zeros_like(o_ref)


def _expert_call(tile_expert, nact, xs, w_gate, w_up, w_down, *, rt):
    n_rows, half = xs.shape
    _, d, de = w_gate.shape

    def row_map(j, te, na):
        return (jnp.minimum(j, na[0] - 1), 0)

    grid_spec = pltpu.PrefetchScalarGridSpec(
        num_scalar_prefetch=2,
        grid=(n_rows // rt,),
        in_specs=[
            pl.BlockSpec((rt, half), row_map),
            pl.BlockSpec((1, d, de), lambda j, te, na: (te[j], 0, 0)),
            pl.BlockSpec((1, d, de), lambda j, te, na: (te[j], 0, 0)),
            pl.BlockSpec((1, de, d), lambda j, te, na: (te[j], 0, 0)),
        ],
        out_specs=pl.BlockSpec((rt, half), lambda j, te, na: (j, 0)),
        scratch_shapes=[pltpu.VMEM((d, de), BF16), pltpu.VMEM((d, de), BF16), pltpu.VMEM((de, d), BF16)],
    )
    return pl.pallas_call(
        _expert_kernel,
        grid_spec=grid_spec,
        out_shape=jax.ShapeDtypeStruct((n_rows, half), jnp.uint32),
        compiler_params=pltpu.CompilerParams(dimension_semantics=("arbitrary",), vmem_limit_bytes=VMEM_LIMIT),
        name="moe_experts",
    )(tile_expert, nact, xs, w_gate, w_up, w_down)


def _combine_kernel(cnt_ref, lbase_ref, toff_ref, x_ref, gt_ref, col_ref, fin_ref, ys_ref, o_ref,
                    buf_ref, sem, *, final):
    i = pl.program_id(0)
    n = pl.num_programs(0)
    cur = i % 2
    tm = x_ref.shape[0]

    nslot = _slot_rows(tm)

    def runs(tile, which, op):
        _tile_runs(tile, cnt_ref, lbase_ref, toff_ref, buf_ref.at[which], ys_ref, sem.at[which], nslot, False, op)

    @pl.when(i == 0)
    def _():
        buf_ref[...] = jnp.zeros_like(buf_ref)
        runs(i, cur, _start)

    @pl.when(i + 1 < n)
    def _():
        runs(i + 1, 1 - cur, _start)

    runs(i, cur, _wait)
    ya, yb = _unpack_pair(buf_ref[cur])
    col = col_ref[...]
    lane = lax.broadcasted_iota(jnp.int32, (tm, nslot), 1).astype(F32)
    unsort = jnp.where(lane == col[:, 0:1], col[:, 2:3],
                       jnp.where(lane == col[:, 1:2], col[:, 3:4], 0.0)).astype(BF16)
    moe = jnp.concatenate([_dot(unsort, ya), _dot(unsort, yb)], axis=1)
    y = x_ref[...] + gt_ref[0] * moe
    if final:
        y = _rms(y) * fin_ref[...]
    o_ref[...] = y


def _combine_call(seg, x2d, gt, cols, norm_final, ys, *, tm, tiles_per_batch, final):
    t, d = x2d.shape
    nt = t // tm
    grid_spec = pltpu.PrefetchScalarGridSpec(
        num_scalar_prefetch=3,
        grid=(nt,),
        in_specs=[
            pl.BlockSpec((tm, d), lambda i, *_: (i, 0)),
            pl.BlockSpec((1, 1, d), lambda i, *_: (i // tiles_per_batch, 0, 0)),
            pl.BlockSpec((tm, N_EXPERTS), lambda i, *_: (i, 0)),
            pl.BlockSpec((1, d), lambda i, *_: (0, 0)),
            pl.BlockSpec(memory_space=pl.ANY),
        ],
        out_specs=pl.BlockSpec((tm, d), lambda i, *_: (i, 0)),
        scratch_shapes=[pltpu.VMEM((2, _slot_rows(tm), d // 2), jnp.uint32), pltpu.SemaphoreType.DMA((2,))],
    )
    return pl.pallas_call(
        functools.partial(_combine_kernel, final=final),
        grid_spec=grid_spec,
        out_shape=jax.ShapeDtypeStruct((t, d), F32),
        compiler_params=pltpu.CompilerParams(dimension_semantics=("arbitrary",), vmem_limit_bytes=VMEM_LIMIT),
        name="moe_combine",
    )(seg["cnt"], seg["lbase"], seg["toff"], x2d, gt, cols, norm_final.reshape(1, d), ys)


def _moe_layer(x, sh, sc, gt, norm_g, router_w, router_b, w_gate, w_up, w_down, norm_final, *, tm, rt, final):
    bsz, s, d = x.shape
    t = bsz * s
    x2d = x.reshape(t, d)
    tiles_per_batch = s // tm
    h2d, slot_rows, cols, cnt = _router_call(x2d, sh, sc, norm_g, router_w, router_b, tm=tm,
                                             tiles_per_batch=tiles_per_batch)
    cnt = cnt[:, :, 0].astype(jnp.int32)
    run = ((cnt + SUBLANES - 1) // SUBLANES) * SUBLANES
    total = jnp.sum(run, axis=0)
    padded = ((total + rt - 1) // rt) * rt
    ends = jnp.cumsum(padded)
    base = ends - padded
    nt = t // tm
    n_rows = -(-(2 * t + nt * N_EXPERTS * SUBLANES) // rt) * rt + N_EXPERTS * rt
    starts = jnp.arange(n_rows // rt, dtype=jnp.int32) * rt
    tile_expert = jnp.minimum(jnp.sum(starts[:, None] >= ends[None, :], axis=1), N_EXPERTS - 1).astype(jnp.int32)
    nact = (ends[-1:] // rt).astype(jnp.int32)
    seg = {
        "cnt": run.reshape(-1),
        "lbase": (jnp.cumsum(run, axis=1) - run).reshape(-1),
        "toff": (base[None, :] + jnp.cumsum(run, axis=0) - run).reshape(-1),
        "padstart": base + total,
        "padlen": padded - total,
        "nact": nact,
    }
    seg = {k: v.astype(jnp.int32) for k, v in seg.items()}
    xs = _dispatch_call(seg, slot_rows, h2d, n_rows, rt=rt)
    ys = _expert_call(tile_expert, nact, xs, w_gate, w_up, w_down, rt=rt)
    out = _combine_call(seg, x2d, gt, cols, norm_final, ys, tm=tm, tiles_per_batch=tiles_per_batch, final=final)
    return out.reshape(bsz, s, d)


def kernel(x, c, ada_w, ada_b, norm_mix, norm_ffn, norm_final, gla_w_in, gla_w_gate2, gla_b_gate2, gla_norm,
           gla_w_out, ssd_w_in, ssd_conv_w, ssd_conv_b, ssd_dt_bias, ssd_a_log, ssd_d, ssd_norm, ssd_w_out,
           router_w, router_b, moe_w_gate, moe_w_up, moe_w_down):
    depth = ada_w.shape[0]
    bsz, s, d = x.shape
    mod = _ada_call(c, ada_w, ada_b).reshape(depth, bsz, 6, 1, d)
    n_mixers = 2
    for i in range(depth):
        sh1, sc1, g1, sh2, sc2, g2 = (mod[i, :, t] for t in range(6))
        j = i // n_mixers
        if i % n_mixers == 0:
            x = _gla_call(x, sh1, sc1, g1, norm_mix[i], gla_w_in[j], gla_w_gate2[j], gla_b_gate2[j],
                          gla_norm[j], gla_w_out[j], blk=min(512, s))
        else:
            x = _ssd_call(x, sh1, sc1, g1, norm_mix[i], ssd_w_in[j], ssd_conv_w[j], ssd_conv_b[j],
                          ssd_dt_bias[j], ssd_a_log[j], ssd_d[j], ssd_norm[j], ssd_w_out[j], blk=min(256, s))
        x = _moe_layer(x, sh2, sc2, g2, norm_ffn[i], router_w, router_b, moe_w_gate[i], moe_w_up[i],
                       moe_w_down[i], norm_final, tm=min(256, s), rt=256, final=(i == depth - 1))
    return x
```

```python
import functools

import jax
import jax.numpy as jnp
from jax import lax
from jax.experimental import pallas as pl
from jax.experimental.pallas import tpu as pltpu

F32 = jnp.float32
BF16 = jnp.bfloat16

EPS = 1e-6
CHUNK = 64

GLA_HEADS = 4
GLA_GATE_RANK = 16
GLA_GATE_TAU = 16.0
GLA_SAFE_DECAY = 40.0

SSD_HEADDIM = 64
SSD_GROUPS = 4
SSD_STATE = 128
SSD_CONV = 4

N_EXPERTS = 16
N_GROUPS = 4
EXPERTS_PER_GROUP = N_EXPERTS // N_GROUPS

LANES = 128
SUBLANES = 8
VMEM_LIMIT = 56 * 1024 * 1024


def _dot(a, b):
    return jnp.dot(a, b, preferred_element_type=F32)


def _dot_nt(a, b):
    return lax.dot_general(a, b, (((1,), (1,)), ((), ())), preferred_element_type=F32)


def _dot_tn(a, b):
    return lax.dot_general(a, b, (((0,), (0,)), ((), ())), preferred_element_type=F32)


def _split2(x):
    hi = x.astype(BF16)
    lo = (x - hi.astype(F32)).astype(BF16)
    return hi, lo


def _dot_hl(a_bf16, x):
    hi, lo = _split2(x)
    return _dot(a_bf16, hi) + _dot(a_bf16, lo)


def _sigmoid(x):
    return 1.0 / (1.0 + jnp.exp(-x))


def _silu(x):
    return x * _sigmoid(x)


def _softplus(x):
    return jnp.maximum(x, 0.0) + jnp.log(1.0 + jnp.exp(-jnp.abs(x)))


def _log_sigmoid(x):
    return jnp.minimum(x, 0.0) - jnp.log(1.0 + jnp.exp(-jnp.abs(x)))


def _rms(x):
    return x * lax.rsqrt(jnp.mean(x * x, axis=-1, keepdims=True) + EPS)


def _ada_norm(x, g, shift, scale):
    return _rms(x) * g * (1.0 + scale) + shift


def _const_spec(shape):
    nd = len(shape)
    return pl.BlockSpec(shape, lambda *_: (0,) * nd, pipeline_mode=pl.Buffered(1))


def _tri(n, dtype):
    r = lax.broadcasted_iota(jnp.int32, (n, n), 0)
    c = lax.broadcasted_iota(jnp.int32, (n, n), 1)
    return (c <= r).astype(dtype)


def _ada_kernel(c_ref, w_ref, b_ref, o_ref):
    cond = _silu(c_ref[...]).astype(BF16)
    o_ref[0] = _dot(cond, w_ref[0].astype(BF16)) + b_ref[0]


def _ada_call(c, ada_w, ada_b):
    depth, d, n = ada_w.shape
    bsz = c.shape[0]
    tn = 1536
    return pl.pallas_call(
        _ada_kernel,
        grid=(depth, n // tn),
        in_specs=[
            pl.BlockSpec((bsz, d), lambda i, j: (0, 0)),
            pl.BlockSpec((1, d, tn), lambda i, j: (i, 0, j)),
            pl.BlockSpec((1, 1, tn), lambda i, j: (i, 0, j)),
        ],
        out_specs=pl.BlockSpec((1, bsz, tn), lambda i, j: (i, 0, j)),
        out_shape=jax.ShapeDtypeStruct((depth, bsz, n), F32),
        compiler_params=pltpu.CompilerParams(
            dimension_semantics=("arbitrary", "arbitrary"), vmem_limit_bytes=VMEM_LIMIT),
        name="ada_mod",
    )(c, ada_w, ada_b.reshape(depth, 1, n))


def _gla_kernel(x_ref, sh_ref, sc_ref, gt_ref, nm_ref, wqkvr_ref, wglr_ref, wg2_ref, bg2_ref,
                hn_ref, wout_ref, o_ref, proj_ref, loga_ref, oacc_ref, state_ref, score_ref, b_ref, *, dk, dv):
    hk = dk // GLA_HEADS
    hv = dv // GLA_HEADS
    blk = x_ref.shape[1]

    @pl.when(pl.program_id(1) == 0)
    def _():
        state_ref[...] = jnp.zeros_like(state_ref)

    x = x_ref[0]
    h = _ada_norm(x, nm_ref[...], sh_ref[0], sc_ref[0]).astype(BF16)
    proj_ref[...] = _dot(h, wqkvr_ref[...])
    glr = _dot(h, wglr_ref[...]).astype(BF16)
    z = _dot(glr, wg2_ref[...]) + bg2_ref[...]
    loga_ref[...] = _log_sigmoid(z) * (1.0 / GLA_GATE_TAU)

    tri = _tri(CHUNK, BF16)
    row = lax.broadcasted_iota(jnp.int32, (CHUNK, CHUNK), 0)
    col = lax.broadcasted_iota(jnp.int32, (CHUNK, CHUNK), 1)
    causal = col <= row
    qscale = hk ** -0.5

    def chunk_body(c, carry):
        r0 = pl.multiple_of(c * CHUNK, CHUNK)
        rows = pl.ds(r0, CHUNK)
        b = _dot_hl(tri, loga_ref[rows, :])
        unsafe = jnp.max(-b[CHUNK - 1:CHUNK, :]) > GLA_SAFE_DECAY
        qts = []
        for hd in range(GLA_HEADS):
            bh = b[:, hd * hk:(hd + 1) * hk]
            q = proj_ref[rows, hd * hk:(hd + 1) * hk] * qscale
            k = proj_ref[rows, dk + hd * hk:dk + (hd + 1) * hk]
            qt = (q * jnp.exp(bh)).astype(BF16)
            kt = (k * jnp.exp(-bh)).astype(BF16)
            qts.append(qt)
            score_ref[hd] = jnp.where(causal, _dot_nt(qt, kt), 0.0)

        @pl.when(unsafe)
        def _():
            b_ref[...] = b
            for hd in range(GLA_HEADS):
                hcols = slice(hd * hk, (hd + 1) * hk)
                q = proj_ref[rows, hcols] * qscale
                bh = b_ref[:, hcols]
                score_ref[hd] = jnp.zeros((CHUNK, CHUNK), F32)

                def columns(g, carry2, hd=hd, hcols=hcols, q=q, bh=bh):
                    s0 = pl.multiple_of(g * SUBLANES, SUBLANES)
                    k8 = proj_ref[pl.ds(r0 + s0, SUBLANES), dk + hd * hk:dk + (hd + 1) * hk]
                    b8 = b_ref[pl.ds(s0, SUBLANES), hcols]
                    sc = score_ref[hd]
                    for u in range(SUBLANES):
                        s = s0 + u
                        decay = jnp.exp(jnp.where(row[:, 0:1] >= s, bh - b8[u:u + 1, :], -jnp.inf))
                        val = jnp.sum(q * k8[u:u + 1, :] * decay, axis=-1, keepdims=True)
                        sc = jnp.where(col == s, val, sc)
                    score_ref[hd] = sc
                    return carry2

                lax.fori_loop(0, CHUNK // SUBLANES, columns, 0)

        for hd in range(GLA_HEADS):
            bh = b[:, hd * hk:(hd + 1) * hk]
            k = proj_ref[rows, dk + hd * hk:dk + (hd + 1) * hk]
            v = proj_ref[rows, 2 * dk + hd * hv:2 * dk + (hd + 1) * hv].astype(BF16)
            b_last = bh[CHUNK - 1:CHUNK, :]
            k_end = (k * jnp.exp(b_last - bh)).astype(BF16)
            st = state_ref[hd]
            o = _dot(score_ref[hd].astype(BF16), v) + _dot_nt(qts[hd], st.astype(BF16))
            state_ref[hd] = jnp.exp(b_last) * st + _dot_tn(v, k_end)
            oacc_ref[rows, hd * hv:(hd + 1) * hv] = _rms(o) * hn_ref[...]
        return carry

    lax.fori_loop(0, blk // CHUNK, chunk_body, 0, unroll=2)

    r = proj_ref[:, 2 * dk + dv:2 * dk + 2 * dv]
    og = (oacc_ref[...] * _silu(r)).astype(BF16)
    o_ref[0] = x + gt_ref[0] * _dot(og, wout_ref[...])


def _gla_call(x, sh, sc, gt, norm_g, w_in, w_gate2, b_gate2, head_norm, w_out, *, blk):
    bsz, s, d = x.shape
    dk = w_gate2.shape[1]
    dv = w_out.shape[0]
    nq = 2 * dk + 2 * dv
    w_qkvr = w_in[:, :nq].astype(BF16)
    w_glr = jnp.pad(w_in[:, nq:], ((0, 0), (0, LANES - GLA_GATE_RANK))).astype(BF16)
    w_g2 = jnp.pad(w_gate2, ((0, LANES - GLA_GATE_RANK), (0, 0))).astype(BF16)
    row3 = pl.BlockSpec((1, 1, d), lambda b, l: (b, 0, 0))
    xspec = pl.BlockSpec((1, blk, d), lambda b, l: (b, l, 0))
    return pl.pallas_call(
        functools.partial(_gla_kernel, dk=dk, dv=dv),
        grid=(bsz, s // blk),
        in_specs=[
            xspec, row3, row3, row3,
            _const_spec((1, d)),
            _const_spec((d, nq)),
            _const_spec((d, LANES)),
            _const_spec((LANES, dk)),
            _const_spec((1, dk)),
            _const_spec((1, dv // GLA_HEADS)),
            _const_spec((dv, d)),
        ],
        out_specs=xspec,
        out_shape=jax.ShapeDtypeStruct((bsz, s, d), F32),
        scratch_shapes=[
            pltpu.VMEM((blk, nq), F32),
            pltpu.VMEM((blk, dk), F32),
            pltpu.VMEM((blk, dv), F32),
            pltpu.VMEM((GLA_HEADS, dv // GLA_HEADS, dk // GLA_HEADS), F32),
            pltpu.VMEM((GLA_HEADS, CHUNK, CHUNK), F32),
            pltpu.VMEM((CHUNK, dk), F32),
        ],
        compiler_params=pltpu.CompilerParams(
            dimension_semantics=("arbitrary", "arbitrary"), vmem_limit_bytes=VMEM_LIMIT),
        name="gla_layer",
    )(x, sh, sc, gt, norm_g.reshape(1, d), w_qkvr, w_glr, w_g2, b_gate2.reshape(1, dk),
      head_norm.reshape(1, -1), w_out.astype(BF16))


def _ssd_kernel(x_ref, sh_ref, sc_ref, gt_ref, nm_ref, wz_ref, wxbc_ref, wdt_ref, cw_ref, cb_ref,
                dtb_ref, alog_ref, dfull_ref, gn_ref, wout_ref, pairsel_ref,
                o_ref, pad_ref, xbc_ref, dt_ref, y_ref, state_ref, *, inner, nheads):
    blk = x_ref.shape[1]
    gs = SSD_GROUPS * SSD_STATE
    gw = inner // SSD_GROUPS
    first = pl.program_id(1) == 0

    @pl.when(first)
    def _():
        state_ref[...] = jnp.zeros_like(state_ref)
        pad_ref[0:SUBLANES, :] = jnp.zeros((SUBLANES, pad_ref.shape[1]), F32)

    x = x_ref[0]
    h = _ada_norm(x, nm_ref[...], sh_ref[0], sc_ref[0]).astype(BF16)

    pad_ref[SUBLANES:SUBLANES + blk, :] = _dot(h, wxbc_ref[...])
    conv = cb_ref[...]
    for kk in range(SSD_CONV):
        off = SUBLANES - (SSD_CONV - 1) + kk
        conv = conv + cw_ref[kk:kk + 1, :] * pad_ref[off:off + blk, :]
    xbc_ref[...] = _silu(conv)
    pad_ref[0:SUBLANES, :] = pad_ref[blk:blk + SUBLANES, :]

    dt_ref[...] = _softplus(_dot(h, wdt_ref[...]) + dtb_ref[...])
    a_neg = -jnp.exp(alog_ref[...])
    lane = lax.broadcasted_iota(jnp.int32, (1, LANES), 1)
    a_neg = jnp.where(lane < nheads, a_neg, 0.0)

    tri = _tri(CHUNK, BF16)
    row2 = lax.broadcasted_iota(jnp.int32, (CHUNK, 2 * CHUNK), 0)
    col2 = lax.broadcasted_iota(jnp.int32, (CHUNK, 2 * CHUNK), 1)
    left = col2 < CHUNK
    causal2 = jnp.where(left, col2, col2 - CHUNK) <= row2
    head_lane = lax.broadcasted_iota(jnp.int32, (CHUNK, LANES), 1)
    even = (head_lane % 2) == 0
    npairs = nheads // 2
    hpg = nheads // SSD_GROUPS

    def chunk_body(c, carry):
        r0 = pl.multiple_of(c * CHUNK, CHUNK)
        rows = pl.ds(r0, CHUNK)
        dt = dt_ref[rows, :]
        acum = _dot_hl(tri, dt * a_neg)
        a_last = acum[CHUNK - 1:CHUNK, :]
        w_end = dt * jnp.exp(a_last - acum)
        def pair_rows(v):
            vv = jnp.concatenate([jnp.where(even, v, 0.0), jnp.where(even, 0.0, v)], axis=0)
            hi, lo = _split2(vv)
            return _dot_nt(pairsel_ref[...], hi) + _dot_nt(pairsel_ref[...], lo)
        acum_t = pair_rows(acum)
        dt_t = pair_rows(dt)

        def pair_cols(v, h0):
            return jnp.where(left, v[:, h0:h0 + 1], v[:, h0 + 1:h0 + 2])

        pw = 2 * SSD_HEADDIM
        for g in range(SSD_GROUPS):
            bm = xbc_ref[rows, inner + g * SSD_STATE:inner + (g + 1) * SSD_STATE].astype(BF16)
            cm = xbc_ref[rows, inner + gs + g * SSD_STATE:inner + gs + (g + 1) * SSD_STATE].astype(BF16)
            cb2 = _dot_nt(cm, jnp.concatenate([bm, bm], axis=0))
            st = state_ref[g]
            c_st = _dot(cm, st.astype(BF16))
            xw_parts, decay_parts = [], []
            for pj in range(hpg // 2):
                j = g * (hpg // 2) + pj
                h0 = 2 * j
                a_col = pair_cols(acum, h0)
                e_col = jnp.exp(a_col)
                xp = xbc_ref[rows, j * pw:(j + 1) * pw]
                xw_parts.append((xp * pair_cols(w_end, h0)).astype(BF16))
                decay_parts.append(e_col[CHUNK - 1:CHUNK, :])
                seg = a_col - acum_t[j:j + 1, :]
                lmat = jnp.exp(jnp.where(causal2, seg, -jnp.inf))
                wgt = (cb2 * lmat * dt_t[j:j + 1, :]).astype(BF16)
                rhs = jnp.concatenate([jnp.where(left, xp, 0.0), jnp.where(left, 0.0, xp)], axis=0)
                y_diag = _dot(wgt, rhs.astype(BF16))
                y_off = e_col * c_st[:, pj * pw:(pj + 1) * pw]
                y_ref[rows, j * pw:(j + 1) * pw] = y_diag + y_off + dfull_ref[:, j * pw:(j + 1) * pw] * xp
            state_ref[g] = (jnp.concatenate(decay_parts, axis=1) * st
                            + _dot_tn(bm, jnp.concatenate(xw_parts, axis=1)))
        return carry

    lax.fori_loop(0, blk // CHUNK, chunk_body, 0, unroll=2)

    z = _dot(h, wz_ref[...])
    y = y_ref[...] * _silu(z)
    yn = jnp.concatenate([_rms(y[:, g * gw:(g + 1) * gw]) for g in range(SSD_GROUPS)], axis=1)
    yn = (yn * gn_ref[...]).astype(BF16)
    o_ref[0] = x + gt_ref[0] * _dot(yn, wout_ref[...])


def _ssd_call(x, sh, sc, gt, norm_g, w_in, conv_w, conv_b, dt_bias, a_log, d_skip, gnorm, w_out, *, blk):
    bsz, s, d = x.shape
    inner = w_out.shape[0]
    nheads = dt_bias.shape[0]
    gs = SSD_GROUPS * SSD_STATE
    cch = inner + 2 * gs
    assert nheads <= LANES and nheads % (2 * SSD_GROUPS) == 0 and 2 * SSD_HEADDIM == LANES
    w_z = w_in[:, :inner].astype(BF16)
    w_xbc = w_in[:, inner:inner + cch].astype(BF16)
    w_dt = jnp.pad(w_in[:, inner + cch:], ((0, 0), (0, LANES - nheads))).astype(BF16)
    dtb = jnp.pad(dt_bias, (0, LANES - nheads)).reshape(1, LANES)
    alog = jnp.pad(a_log, (0, LANES - nheads)).reshape(1, LANES)
    d_full = jnp.repeat(d_skip, SSD_HEADDIM).reshape(1, inner)
    npair_pad = max(SUBLANES, nheads // 2)
    pairsel = ((jnp.arange(LANES)[None, :] // 2) == jnp.arange(npair_pad)[:, None]).astype(BF16)
    row3 = pl.BlockSpec((1, 1, d), lambda b, l: (b, 0, 0))
    xspec = pl.BlockSpec((1, blk, d), lambda b, l: (b, l, 0))
    return pl.pallas_call(
        functools.partial(_ssd_kernel, inner=inner, nheads=nheads),
        grid=(bsz, s // blk),
        in_specs=[
            xspec, row3, row3, row3,
            _const_spec((1, d)),
            _const_spec((d, inner)),
            _const_spec((d, cch)),
            _const_spec((d, LANES)),
            _const_spec((SSD_CONV, cch)),
            _const_spec((1, cch)),
            _const_spec((1, LANES)),
            _const_spec((1, LANES)),
            _const_spec((1, inner)),
            _const_spec((1, inner)),
            _const_spec((inner, d)),
            _const_spec((npair_pad, LANES)),
        ],
        out_specs=xspec,
        out_shape=jax.ShapeDtypeStruct((bsz, s, d), F32),
        scratch_shapes=[
            pltpu.VMEM((blk + 2 * SUBLANES, cch), F32),
            pltpu.VMEM((blk, cch), F32),
            pltpu.VMEM((blk, LANES), F32),
            pltpu.VMEM((blk, inner), F32),
            pltpu.VMEM((SSD_GROUPS, SSD_STATE, inner // SSD_GROUPS), F32),
        ],
        compiler_params=pltpu.CompilerParams(
            dimension_semantics=("arbitrary", "arbitrary"), vmem_limit_bytes=VMEM_LIMIT),
        name="ssd_layer",
    )(x, sh, sc, gt, norm_g.reshape(1, d), w_z, w_xbc, w_dt, conv_w, conv_b.reshape(1, cch),
      dtb, alog, d_full, gnorm.reshape(1, inner), w_out.astype(BF16), pairsel)


def _route(logits_t, rb_ref):
    scores = _sigmoid(logits_t)
    sc = [scores[e:e + 1, :] for e in range(N_EXPERTS)]
    bi = [sc[e] + rb_ref[e:e + 1, :] for e in range(N_EXPERTS)]
    gscore = []
    for g in range(N_GROUPS):
        a, b, c, d = bi[4 * g:4 * g + 4]
        hi1, lo1 = jnp.maximum(a, b), jnp.minimum(a, b)
        hi2, lo2 = jnp.maximum(c, d), jnp.minimum(c, d)
        gscore.append(jnp.maximum(hi1, hi2) + jnp.maximum(jnp.minimum(hi1, hi2), jnp.maximum(lo1, lo2)))
    gates = []
    flags = []
    for g in range(N_GROUPS):
        gsel = None
        for o in range(N_GROUPS):
            if o == g:
                continue
            win = (gscore[g] >= gscore[o]) if o > g else (gscore[g] > gscore[o])
            gsel = win if gsel is None else jnp.logical_and(gsel, win)
        sel = []
        for i in range(EXPERTS_PER_GROUP):
            e = 4 * g + i
            rank = jnp.zeros_like(bi[e])
            for j in range(EXPERTS_PER_GROUP):
                if j == i:
                    continue
                o = 4 * g + j
                beats = (bi[o] >= bi[e]) if j < i else (bi[o] > bi[e])
                rank = rank + jnp.where(beats, 1.0, 0.0)
            chosen = jnp.logical_and(gsel, rank < 2.0)
            flags.append(jnp.where(chosen, 1.0, 0.0))
            sel.append(jnp.where(chosen, sc[e], 0.0))
        denom = sel[0] + sel[1] + sel[2] + sel[3]
        denom = jnp.where(gsel, denom, 1.0)
        gates.extend([s_ / denom for s_ in sel])
    return gates, flags


def _router_kernel(x_ref, sh_ref, sc_ref, nm_ref, rwt_ref, rb_ref, upper_ref,
                   h_ref, slot_ref, col_ref, cnt_ref):
    tm = x_ref.shape[0]
    hb = _ada_norm(x_ref[...], nm_ref[...], sh_ref[0], sc_ref[0]).astype(BF16)
    h_ref[...] = hb
    gates, flags = _route(_dot_nt(rwt_ref[...], hb), rb_ref)
    big = float(N_EXPERTS)
    e_lo = functools.reduce(jnp.minimum, [jnp.where(flags[e] > 0.0, float(e), big) for e in range(N_EXPERTS)])
    e_hi = functools.reduce(jnp.maximum, [jnp.where(flags[e] > 0.0, float(e), -1.0) for e in range(N_EXPERTS)])
    flag_mat = jnp.concatenate(flags, axis=0)
    prefix = _dot(flag_mat.astype(BF16), upper_ref[...])
    counts = jnp.sum(flag_mat, axis=1, keepdims=True)
    slot_of = []
    before = jnp.zeros((1, 1), F32)
    for e in range(N_EXPERTS):
        slot_of.append(prefix[e:e + 1, :] + before)
        before = before + jnp.floor((counts[e:e + 1, :] + (SUBLANES - 1)) * (1.0 / SUBLANES)) * SUBLANES

    def pick(eid, rows):
        return functools.reduce(
            jnp.add, [jnp.where(eid == float(e), rows[e], 0.0) for e in range(N_EXPERTS)])

    zero = jnp.zeros((1, tm), F32)
    s_lo, s_hi = pick(e_lo, slot_of), pick(e_hi, slot_of)
    slot_ref[0] = jnp.concatenate([s_lo, s_hi] + [zero] * (SUBLANES - 2), axis=0).astype(jnp.int32)
    rows = jnp.concatenate([s_lo, s_hi, pick(e_lo, gates), pick(e_hi, gates)] + [zero] * (N_EXPERTS - 4), axis=0)
    col_ref[...] = rows.T
    cnt_ref[0] = jnp.broadcast_to(counts, (N_EXPERTS, LANES))


def _router_call(x2d, sh, sc, norm_g, router_w, router_b, *, tm, tiles_per_batch):
    t, d = x2d.shape
    nt = t // tm
    row3 = pl.BlockSpec((1, 1, d), lambda i: (i // tiles_per_batch, 0, 0))
    rb = jnp.broadcast_to(router_b.reshape(N_EXPERTS, 1), (N_EXPERTS, tm))
    upper = (jnp.arange(tm)[:, None] < jnp.arange(tm)[None, :]).astype(BF16)
    return pl.pallas_call(
        _router_kernel,
        grid=(nt,),
        in_specs=[
            pl.BlockSpec((tm, d), lambda i: (i, 0)), row3, row3,
            _const_spec((1, d)),
            _const_spec((N_EXPERTS, d)),
            _const_spec((N_EXPERTS, tm)),
            _const_spec((tm, tm)),
        ],
        out_specs=[
            pl.BlockSpec((tm, d), lambda i: (i, 0)),
            pl.BlockSpec((1, SUBLANES, tm), lambda i: (i, 0, 0)),
            pl.BlockSpec((tm, N_EXPERTS), lambda i: (i, 0)),
            pl.BlockSpec((1, N_EXPERTS, LANES), lambda i: (i, 0, 0)),
        ],
        out_shape=[
            jax.ShapeDtypeStruct((t, d), BF16),
            jax.ShapeDtypeStruct((nt, SUBLANES, tm), jnp.int32),
            jax.ShapeDtypeStruct((t, N_EXPERTS), F32),
            jax.ShapeDtypeStruct((nt, N_EXPERTS, LANES), F32),
        ],
        compiler_params=pltpu.CompilerParams(dimension_semantics=("arbitrary",), vmem_limit_bytes=VMEM_LIMIT),
        name="moe_router",
    )(x2d, sh, sc, norm_g.reshape(1, d), router_w.T.astype(BF16), rb, upper)


def _pack_pair(a, b):
    ua = lax.bitcast_convert_type(a, jnp.uint32)
    ub = lax.bitcast_convert_type(b, jnp.uint32)
    return (ua >> 16) | (ub & jnp.uint32(0xFFFF0000))


def _unpack_pair(w):
    a = lax.bitcast_convert_type(w << 16, F32)
    b = lax.bitcast_convert_type(w & jnp.uint32(0xFFFF0000), F32)
    return a.astype(BF16), b.astype(BF16)


def _bf16_exact(x):
    return x.astype(BF16).astype(F32)


def _slot_rows(tm):
    return 2 * tm + N_EXPERTS * SUBLANES


def _run_copies(n_rows, make_copy, op):
    def piece(k, carry):
        op(make_copy(pl.multiple_of(k * SUBLANES, SUBLANES)))
        return carry

    lax.fori_loop(0, n_rows >> (SUBLANES.bit_length() - 1), piece, 0)


def _tile_runs(tile, len_ref, lbase_ref, toff_ref, local_ref, global_ref, sem, to_global, op):
    for e in range(N_EXPERTS):
        idx = tile * N_EXPERTS + e
        lo = lbase_ref[idx]
        go = toff_ref[idx]

        def make_copy(offset, lo=lo, go=go):
            loc = local_ref.at[pl.ds(pl.multiple_of(lo + offset, SUBLANES), SUBLANES)]
            glb = global_ref.at[pl.ds(pl.multiple_of(go + offset, SUBLANES), SUBLANES)]
            return pltpu.make_async_copy(loc, glb, sem) if to_global else pltpu.make_async_copy(glb, loc, sem)

        _run_copies(len_ref[idx], make_copy, op)


def _start(copy):
    copy.start()


def _wait(copy):
    copy.wait()


def _dispatch_kernel(cnt_ref, lbase_ref, toff_ref, padstart_ref, padlen_ref, nact_ref, slot_ref, h_ref, xs_ref,
                     buf_ref, zero_ref, sem, zsem, *, rt):
    i = pl.program_id(0)
    last = pl.num_programs(0) - 1
    cur = i % 2
    tm = h_ref.shape[0]
    half = h_ref.shape[1] // 2

    nslot = _slot_rows(tm)

    def runs(tile, which, op):
        _tile_runs(tile, cnt_ref, lbase_ref, toff_ref, buf_ref.at[which], xs_ref, sem.at[which], True, op)

    @pl.when(i > 0)
    def _():
        runs(i - 1, 1 - cur, _wait)

    sl = slot_ref[0]
    srow = lax.broadcasted_iota(jnp.int32, (nslot, tm), 0)
    perm = jnp.where(srow == sl[0:1, :], 1.0, jnp.where(srow == sl[1:2, :], 1.0, 0.0)).astype(BF16)
    rows = _dot(perm, h_ref[...])
    buf_ref[cur] = _pack_pair(rows[:, :half], rows[:, half:])
    runs(i, cur, _start)

    @pl.when(i == last)
    def _():
        runs(i, cur, _wait)
        zero_ref[...] = jnp.zeros_like(zero_ref)
        for op in (_start, _wait):
            for e in range(N_EXPERTS):
                def make_copy(offset, e=e):
                    dst = xs_ref.at[pl.ds(pl.multiple_of(padstart_ref[e] + offset, SUBLANES), SUBLANES)]
                    return pltpu.make_async_copy(zero_ref.at[pl.ds(0, SUBLANES)], dst, zsem)
                _run_copies(padlen_ref[e], make_copy, op)

        def zero_tile(j, carry):
            copy = pltpu.make_async_copy(zero_ref, xs_ref.at[pl.ds(pl.multiple_of(j * rt, rt), rt)], zsem)
            copy.start()
            copy.wait()
            return carry

        lax.fori_loop(nact_ref[0], xs_ref.shape[0] // rt, zero_tile, 0)


def _dispatch_call(seg, slot_rows, h2d, n_rows, *, rt):
    t, d = h2d.shape
    nt, _, tm = slot_rows.shape
    grid_spec = pltpu.PrefetchScalarGridSpec(
        num_scalar_prefetch=6,
        grid=(nt,),
        in_specs=[
            pl.BlockSpec((1, SUBLANES, tm), lambda i, *_: (i, 0, 0)),
            pl.BlockSpec((tm, d), lambda i, *_: (i, 0)),
        ],
        out_specs=pl.BlockSpec(memory_space=pl.ANY),
        scratch_shapes=[
            pltpu.VMEM((2, _slot_rows(tm), d // 2), jnp.uint32),
            pltpu.VMEM((rt, d // 2), jnp.uint32),
            pltpu.SemaphoreType.DMA((2,)),
            pltpu.SemaphoreType.DMA,
        ],
    )
    return pl.pallas_call(
        functools.partial(_dispatch_kernel, rt=rt),
        grid_spec=grid_spec,
        out_shape=jax.ShapeDtypeStruct((n_rows, d // 2), jnp.uint32),
        compiler_params=pltpu.CompilerParams(dimension_semantics=("arbitrary",), vmem_limit_bytes=VMEM_LIMIT),
        name="moe_dispatch",
    )(seg["cnt"], seg["lbase"], seg["toff"], seg["padstart"], seg["padlen"], seg["nact"], slot_rows, h2d)


def _expert_kernel(te_ref, nact_ref, xs_ref, wg_ref, wu_ref, wd_ref, o_ref, wgb_ref, wub_ref, wdb_ref):
    j = pl.program_id(0)
    active = j < nact_ref[0]
    half = xs_ref.shape[1]

    @pl.when(jnp.logical_and(active, jnp.logical_or(j == 0, te_ref[j] != te_ref[jnp.maximum(j - 1, 0)])))
    def _():
        wgb_ref[...] = wg_ref[0, 0].astype(BF16)
        wub_ref[...] = wu_ref[0, 0].astype(BF16)
        wdb_ref[...] = wd_ref[0, 0].astype(BF16)

    @pl.when(active)
    def _():
        xa, xb = _unpack_pair(xs_ref[...])
        gate = _dot(xa, wgb_ref[0:half, :]) + _dot(xb, wgb_ref[half:, :])
        up = _dot(xa, wub_ref[0:half, :]) + _dot(xb, wub_ref[half:, :])
        he = (_silu(gate) * up).astype(BF16)
        y = _dot(he, wdb_ref[...])
        o_ref[...] = _pack_pair(_bf16_exact(y[:, :half]), _bf16_exact(y[:, half:]))

    @pl.when(jnp.logical_not(active))
    def _():
        o_ref[...] = jnp.zeros_like(o_ref)


def _expert_call(tile_expert, nact, xs, w_gate, w_up, w_down, *, layer, rt):
    n_rows, half = xs.shape
    _, _, d, de = w_gate.shape

    def row_map(j, te, na):
        return (jnp.minimum(j, na[0] - 1), 0)

    grid_spec = pltpu.PrefetchScalarGridSpec(
        num_scalar_prefetch=2,
        grid=(n_rows // rt,),
        in_specs=[
            pl.BlockSpec((rt, half), row_map),
            pl.BlockSpec((1, 1, d, de), lambda j, te, na: (layer, te[j], 0, 0)),
            pl.BlockSpec((1, 1, d, de), lambda j, te, na: (layer, te[j], 0, 0)),
            pl.BlockSpec((1, 1, de, d), lambda j, te, na: (layer, te[j], 0, 0)),
        ],
        out_specs=pl.BlockSpec((rt, half), lambda j, te, na: (j, 0)),
        scratch_shapes=[pltpu.VMEM((d, de), BF16), pltpu.VMEM((d, de), BF16), pltpu.VMEM((de, d), BF16)],
    )
    return pl.pallas_call(
        _expert_kernel,
        grid_spec=grid_spec,
        out_shape=jax.ShapeDtypeStruct((n_rows, half), jnp.uint32),
        compiler_params=pltpu.CompilerParams(dimension_semantics=("arbitrary",), vmem_limit_bytes=VMEM_LIMIT),
        name="moe_experts",
    )(tile_expert, nact, xs, w_gate, w_up, w_down)


def _combine_kernel(cnt_ref, lbase_ref, toff_ref, x_ref, gt_ref, col_ref, fin_ref, ys_ref, o_ref,
                    buf_ref, sem, *, final):
    i = pl.program_id(0)
    n = pl.num_programs(0)
    cur = i % 2
    tm = x_ref.shape[0]

    nslot = _slot_rows(tm)

    def runs(tile, which, op):
        _tile_runs(tile, cnt_ref, lbase_ref, toff_ref, buf_ref.at[which], ys_ref, sem.at[which], False, op)

    @pl.when(i == 0)
    def _():
        buf_ref[...] = jnp.zeros_like(buf_ref)
        runs(i, cur, _start)

    @pl.when(i + 1 < n)
    def _():
        runs(i + 1, 1 - cur, _start)

    runs(i, cur, _wait)
    ya, yb = _unpack_pair(buf_ref[cur])
    col = col_ref[...]
    lane = lax.broadcasted_iota(jnp.int32, (tm, nslot), 1).astype(F32)
    unsort = jnp.where(lane == col[:, 0:1], col[:, 2:3],
                       jnp.where(lane == col[:, 1:2], col[:, 3:4], 0.0)).astype(BF16)
    moe = jnp.concatenate([_dot(unsort, ya), _dot(unsort, yb)], axis=1)
    y = x_ref[...] + gt_ref[0] * moe
    if final:
        y = _rms(y) * fin_ref[...]
    o_ref[...] = y


def _combine_call(seg, x2d, gt, cols, norm_final, ys, *, tm, tiles_per_batch, final):
    t, d = x2d.shape
    nt = t // tm
    grid_spec = pltpu.PrefetchScalarGridSpec(
        num_scalar_prefetch=3,
        grid=(nt,),
        in_specs=[
            pl.BlockSpec((tm, d), lambda i, *_: (i, 0)),
            pl.BlockSpec((1, 1, d), lambda i, *_: (i // tiles_per_batch, 0, 0)),
            pl.BlockSpec((tm, N_EXPERTS), lambda i, *_: (i, 0)),
            pl.BlockSpec((1, d), lambda i, *_: (0, 0)),
            pl.BlockSpec(memory_space=pl.ANY),
        ],
        out_specs=pl.BlockSpec((tm, d), lambda i, *_: (i, 0)),
        scratch_shapes=[pltpu.VMEM((2, _slot_rows(tm), d // 2), jnp.uint32), pltpu.SemaphoreType.DMA((2,))],
    )
    return pl.pallas_call(
        functools.partial(_combine_kernel, final=final),
        grid_spec=grid_spec,
        out_shape=jax.ShapeDtypeStruct((t, d), F32),
        compiler_params=pltpu.CompilerParams(dimension_semantics=("arbitrary",), vmem_limit_bytes=VMEM_LIMIT),
        name="moe_combine",
    )(seg["cnt"], seg["lbase"], seg["toff"], x2d, gt, cols, norm_final.reshape(1, d), ys)


def _moe_layer(x, sh, sc, gt, norm_g, router_w, router_b, w_gate, w_up, w_down, norm_final, *,
               layer, tm, rt, final):
    bsz, s, d = x.shape
    t = bsz * s
    x2d = x.reshape(t, d)
    tiles_per_batch = s // tm
    h2d, slot_rows, cols, cnt = _router_call(x2d, sh, sc, norm_g, router_w, router_b, tm=tm,
                                             tiles_per_batch=tiles_per_batch)
    cnt = cnt[:, :, 0].astype(jnp.int32)
    run = ((cnt + SUBLANES - 1) // SUBLANES) * SUBLANES
    total = jnp.sum(run, axis=0)
    padded = ((total + rt - 1) // rt) * rt
    ends = jnp.cumsum(padded)
    base = ends - padded
    nt = t // tm
    n_rows = -(-(2 * t + nt * N_EXPERTS * SUBLANES) // rt) * rt + N_EXPERTS * rt
    starts = jnp.arange(n_rows // rt, dtype=jnp.int32) * rt
    tile_expert = jnp.minimum(jnp.sum(starts[:, None] >= ends[None, :], axis=1), N_EXPERTS - 1).astype(jnp.int32)
    nact = (ends[-1:] // rt).astype(jnp.int32)
    seg = {
        "cnt": run.reshape(-1),
        "lbase": (jnp.cumsum(run, axis=1) - run).reshape(-1),
        "toff": (base[None, :] + jnp.cumsum(run, axis=0) - run).reshape(-1),
        "padstart": base + total,
        "padlen": padded - total,
        "nact": nact,
    }
    seg = {k: v.astype(jnp.int32) for k, v in seg.items()}
    xs = _dispatch_call(seg, slot_rows, h2d, n_rows, rt=rt)
    ys = _expert_call(tile_expert, nact, xs, w_gate, w_up, w_down, layer=layer, rt=rt)
    out = _combine_call(seg, x2d, gt, cols, norm_final, ys, tm=tm, tiles_per_batch=tiles_per_batch, final=final)
    return out.reshape(bsz, s, d)


def kernel(x, c, ada_w, ada_b, norm_mix, norm_ffn, norm_final, gla_w_in, gla_w_gate2, gla_b_gate2, gla_norm,
           gla_w_out, ssd_w_in, ssd_conv_w, ssd_conv_b, ssd_dt_bias, ssd_a_log, ssd_d, ssd_norm, ssd_w_out,
           router_w, router_b, moe_w_gate, moe_w_up, moe_w_down):
    depth = ada_w.shape[0]
    bsz, s, d = x.shape
    mod = _ada_call(c, ada_w, ada_b).reshape(depth, bsz, 6, 1, d)
    n_mixers = 2
    for i in range(depth):
        sh1, sc1, g1, sh2, sc2, g2 = (mod[i, :, t] for t in range(6))
        j = i // n_mixers
        if i % n_mixers == 0:
            x = _gla_call(x, sh1, sc1, g1, norm_mix[i], gla_w_in[j], gla_w_gate2[j], gla_b_gate2[j],
                          gla_norm[j], gla_w_out[j], blk=min(512, s))
        else:
            x = _ssd_call(x, sh1, sc1, g1, norm_mix[i], ssd_w_in[j], ssd_conv_w[j], ssd_conv_b[j],
                          ssd_dt_bias[j], ssd_a_log[j], ssd_d[j], ssd_norm[j], ssd_w_out[j], blk=min(256, s))
        x = _moe_layer(x, sh2, sc2, g2, norm_ffn[i], router_w, router_b, moe_w_gate, moe_w_up, moe_w_down,
                       norm_final, layer=i, tm=min(256, s), rt=512, final=(i == depth - 1))
    return x
```

```python
import functools

import jax
import jax.numpy as jnp
from jax import lax
from jax.experimental import pallas as pl
from jax.experimental.pallas import tpu as pltpu

F32 = jnp.float32
BF16 = jnp.bfloat16

EPS = 1e-6
CHUNK = 64

GLA_HEADS = 4
GLA_GATE_RANK = 16
GLA_GATE_TAU = 16.0
GLA_SAFE_DECAY = 40.0

SSD_HEADDIM = 64
SSD_GROUPS = 4
SSD_STATE = 128
SSD_CONV = 4

N_EXPERTS = 16
N_GROUPS = 4
EXPERTS_PER_GROUP = N_EXPERTS // N_GROUPS

LANES = 128
SUBLANES = 8
VMEM_LIMIT = 56 * 1024 * 1024


def _dot(a, b):
    return jnp.dot(a, b, preferred_element_type=F32)


def _dot_nt(a, b):
    return lax.dot_general(a, b, (((1,), (1,)), ((), ())), preferred_element_type=F32)


def _dot_tn(a, b):
    return lax.dot_general(a, b, (((0,), (0,)), ((), ())), preferred_element_type=F32)


def _split2(x):
    hi = x.astype(BF16)
    lo = (x - hi.astype(F32)).astype(BF16)
    return hi, lo


def _dot_hl(a_bf16, x):
    hi, lo = _split2(x)
    return _dot(a_bf16, hi) + _dot(a_bf16, lo)


def _sigmoid(x):
    return 1.0 / (1.0 + jnp.exp(-x))


def _silu(x):
    return x * (0.5 * jnp.tanh(0.5 * x) + 0.5)


def _softplus(x):
    return jnp.maximum(x, 0.0) + jnp.log(1.0 + jnp.exp(-jnp.abs(x)))


def _log_sigmoid(x):
    return jnp.minimum(x, 0.0) - jnp.log(1.0 + jnp.exp(-jnp.abs(x)))


def _rms(x):
    return x * lax.rsqrt(jnp.mean(x * x, axis=-1, keepdims=True) + EPS)


def _ada_norm(x, g, shift, scale):
    return _rms(x) * g * (1.0 + scale) + shift


def _const_spec(shape):
    nd = len(shape)
    return pl.BlockSpec(shape, lambda *_: (0,) * nd, pipeline_mode=pl.Buffered(1))


def _tri(n, dtype):
    r = lax.broadcasted_iota(jnp.int32, (n, n), 0)
    c = lax.broadcasted_iota(jnp.int32, (n, n), 1)
    return (c <= r).astype(dtype)


def _ada_kernel(c_ref, w_ref, b_ref, o_ref):
    cond = _silu(c_ref[...]).astype(BF16)
    o_ref[0] = _dot(cond, w_ref[0].astype(BF16)) + b_ref[0]


def _ada_call(c, ada_w, ada_b):
    depth, d, n = ada_w.shape
    bsz = c.shape[0]
    tn = 1536
    return pl.pallas_call(
        _ada_kernel,
        grid=(depth, n // tn),
        in_specs=[
            pl.BlockSpec((bsz, d), lambda i, j: (0, 0)),
            pl.BlockSpec((1, d, tn), lambda i, j: (i, 0, j)),
            pl.BlockSpec((1, 1, tn), lambda i, j: (i, 0, j)),
        ],
        out_specs=pl.BlockSpec((1, bsz, tn), lambda i, j: (i, 0, j)),
        out_shape=jax.ShapeDtypeStruct((depth, bsz, n), F32),
        compiler_params=pltpu.CompilerParams(
            dimension_semantics=("arbitrary", "arbitrary"), vmem_limit_bytes=VMEM_LIMIT),
        name="ada_mod",
    )(c, ada_w, ada_b.reshape(depth, 1, n))


def _gla_kernel(x_ref, sh_ref, sc_ref, gt_ref, nm_ref, wqkvr_ref, wglr_ref, wg2_ref, bg2_ref,
                hn_ref, wout_ref, o_ref, proj_ref, loga_ref, oacc_ref, state_ref, score_ref, b_ref, *, dk, dv):
    hk = dk // GLA_HEADS
    hv = dv // GLA_HEADS
    blk = x_ref.shape[1]

    @pl.when(pl.program_id(1) == 0)
    def _():
        state_ref[...] = jnp.zeros_like(state_ref)

    x = x_ref[0]
    h = _ada_norm(x, nm_ref[...], sh_ref[0], sc_ref[0]).astype(BF16)
    proj_ref[...] = _dot(h, wqkvr_ref[...])
    glr = _dot(h, wglr_ref[...]).astype(BF16)
    z = _dot(glr, wg2_ref[...]) + bg2_ref[...]
    loga_ref[...] = _log_sigmoid(z) * (1.0 / GLA_GATE_TAU)

    tri = _tri(CHUNK, BF16)
    row = lax.broadcasted_iota(jnp.int32, (CHUNK, CHUNK), 0)
    col = lax.broadcasted_iota(jnp.int32, (CHUNK, CHUNK), 1)
    causal = col <= row
    qscale = hk ** -0.5

    def chunk_body(c, carry):
        r0 = pl.multiple_of(c * CHUNK, CHUNK)
        rows = pl.ds(r0, CHUNK)
        b = _dot_hl(tri, loga_ref[rows, :])
        unsafe = jnp.max(-b[CHUNK - 1:CHUNK, :]) > GLA_SAFE_DECAY
        qts = []
        for hd in range(GLA_HEADS):
            bh = b[:, hd * hk:(hd + 1) * hk]
            q = proj_ref[rows, hd * hk:(hd + 1) * hk] * qscale
            k = proj_ref[rows, dk + hd * hk:dk + (hd + 1) * hk]
            qt = (q * jnp.exp(bh)).astype(BF16)
            kt = (k * jnp.exp(-bh)).astype(BF16)
            qts.append(qt)
            score_ref[hd] = jnp.where(causal, _dot_nt(qt, kt), 0.0)

        @pl.when(unsafe)
        def _():
            b_ref[...] = b
            for hd in range(GLA_HEADS):
                hcols = slice(hd * hk, (hd + 1) * hk)
                q = proj_ref[rows, hcols] * qscale
                bh = b_ref[:, hcols]
                score_ref[hd] = jnp.zeros((CHUNK, CHUNK), F32)

                def columns(g, carry2, hd=hd, hcols=hcols, q=q, bh=bh):
                    s0 = pl.multiple_of(g * SUBLANES, SUBLANES)
                    k8 = proj_ref[pl.ds(r0 + s0, SUBLANES), dk + hd * hk:dk + (hd + 1) * hk]
                    b8 = b_ref[pl.ds(s0, SUBLANES), hcols]
                    sc = score_ref[hd]
                    for u in range(SUBLANES):
                        s = s0 + u
                        decay = jnp.exp(jnp.where(row[:, 0:1] >= s, bh - b8[u:u + 1, :], -jnp.inf))
                        val = jnp.sum(q * k8[u:u + 1, :] * decay, axis=-1, keepdims=True)
                        sc = jnp.where(col == s, val, sc)
                    score_ref[hd] = sc
                    return carry2

                lax.fori_loop(0, CHUNK // SUBLANES, columns, 0)

        for hd in range(GLA_HEADS):
            bh = b[:, hd * hk:(hd + 1) * hk]
            k = proj_ref[rows, dk + hd * hk:dk + (hd + 1) * hk]
            v = proj_ref[rows, 2 * dk + hd * hv:2 * dk + (hd + 1) * hv].astype(BF16)
            b_last = bh[CHUNK - 1:CHUNK, :]
            k_end = (k * jnp.exp(b_last - bh)).astype(BF16)
            st = state_ref[hd]
            o = _dot(score_ref[hd].astype(BF16), v) + _dot_nt(qts[hd], st.astype(BF16))
            state_ref[hd] = jnp.exp(b_last) * st + _dot_tn(v, k_end)
            oacc_ref[rows, hd * hv:(hd + 1) * hv] = _rms(o) * hn_ref[...]
        return carry

    lax.fori_loop(0, blk // CHUNK, chunk_body, 0, unroll=4)

    r = proj_ref[:, 2 * dk + dv:2 * dk + 2 * dv]
    og = (oacc_ref[...] * _silu(r)).astype(BF16)
    o_ref[0] = x + gt_ref[0] * _dot(og, wout_ref[...])


def _gla_call(x, sh, sc, gt, norm_g, w_in, w_gate2, b_gate2, head_norm, w_out, *, blk):
    bsz, s, d = x.shape
    dk = w_gate2.shape[1]
    dv = w_out.shape[0]
    nq = 2 * dk + 2 * dv
    w_qkvr = w_in[:, :nq].astype(BF16)
    w_glr = jnp.pad(w_in[:, nq:], ((0, 0), (0, LANES - GLA_GATE_RANK))).astype(BF16)
    w_g2 = jnp.pad(w_gate2, ((0, LANES - GLA_GATE_RANK), (0, 0))).astype(BF16)
    row3 = pl.BlockSpec((1, 1, d), lambda b, l: (b, 0, 0))
    xspec = pl.BlockSpec((1, blk, d), lambda b, l: (b, l, 0))
    return pl.pallas_call(
        functools.partial(_gla_kernel, dk=dk, dv=dv),
        grid=(bsz, s // blk),
        in_specs=[
            xspec, row3, row3, row3,
            _const_spec((1, d)),
            _const_spec((d, nq)),
            _const_spec((d, LANES)),
            _const_spec((LANES, dk)),
            _const_spec((1, dk)),
            _const_spec((1, dv // GLA_HEADS)),
            _const_spec((dv, d)),
        ],
        out_specs=xspec,
        out_shape=jax.ShapeDtypeStruct((bsz, s, d), F32),
        scratch_shapes=[
            pltpu.VMEM((blk, nq), F32),
            pltpu.VMEM((blk, dk), F32),
            pltpu.VMEM((blk, dv), F32),
            pltpu.VMEM((GLA_HEADS, dv // GLA_HEADS, dk // GLA_HEADS), F32),
            pltpu.VMEM((GLA_HEADS, CHUNK, CHUNK), F32),
            pltpu.VMEM((CHUNK, dk), F32),
        ],
        compiler_params=pltpu.CompilerParams(
            dimension_semantics=("arbitrary", "arbitrary"), vmem_limit_bytes=VMEM_LIMIT),
        name="gla_layer",
    )(x, sh, sc, gt, norm_g.reshape(1, d), w_qkvr, w_glr, w_g2, b_gate2.reshape(1, dk),
      head_norm.reshape(1, -1), w_out.astype(BF16))


def _ssd_kernel(x_ref, sh_ref, sc_ref, gt_ref, nm_ref, wz_ref, wxbc_ref, wdt_ref, cw_ref, cb_ref,
                dtb_ref, alog_ref, dfull_ref, gn_ref, wout_ref, pairsel_ref,
                o_ref, pad_ref, xbc_ref, dt_ref, y_ref, state_ref, *, inner, nheads):
    blk = x_ref.shape[1]
    gs = SSD_GROUPS * SSD_STATE
    gw = inner // SSD_GROUPS
    first = pl.program_id(1) == 0

    @pl.when(first)
    def _():
        state_ref[...] = jnp.zeros_like(state_ref)
        pad_ref[0:SUBLANES, :] = jnp.zeros((SUBLANES, pad_ref.shape[1]), F32)

    x = x_ref[0]
    h = _ada_norm(x, nm_ref[...], sh_ref[0], sc_ref[0]).astype(BF16)

    pad_ref[SUBLANES:SUBLANES + blk, :] = _dot(h, wxbc_ref[...])
    conv = cb_ref[...] + cw_ref[SSD_CONV - 1:SSD_CONV, :] * pad_ref[SUBLANES:SUBLANES + blk, :]
    padded = pad_ref[0:SUBLANES + blk, :]
    for kk in range(SSD_CONV - 1):
        conv = conv + cw_ref[kk:kk + 1, :] * pltpu.roll(padded, SSD_CONV - 1 - kk, 0)[SUBLANES:SUBLANES + blk, :]
    xbc_ref[...] = _silu(conv)
    pad_ref[0:SUBLANES, :] = pad_ref[blk:blk + SUBLANES, :]

    dt_ref[...] = _softplus(_dot(h, wdt_ref[...]) + dtb_ref[...])
    a_neg = -jnp.exp(alog_ref[...])
    lane = lax.broadcasted_iota(jnp.int32, (1, LANES), 1)
    a_neg = jnp.where(lane < nheads, a_neg, 0.0)

    tri = _tri(CHUNK, BF16)
    row2 = lax.broadcasted_iota(jnp.int32, (CHUNK, 2 * CHUNK), 0)
    col2 = lax.broadcasted_iota(jnp.int32, (CHUNK, 2 * CHUNK), 1)
    left = col2 < CHUNK
    causal2 = jnp.where(left, col2, col2 - CHUNK) <= row2
    head_lane = lax.broadcasted_iota(jnp.int32, (CHUNK, LANES), 1)
    even = (head_lane % 2) == 0
    npairs = nheads // 2
    hpg = nheads // SSD_GROUPS

    def chunk_body(c, carry):
        r0 = pl.multiple_of(c * CHUNK, CHUNK)
        rows = pl.ds(r0, CHUNK)
        dt = dt_ref[rows, :]
        acum = _dot_hl(tri, dt * a_neg)
        a_last = acum[CHUNK - 1:CHUNK, :]
        w_end = dt * jnp.exp(a_last - acum)
        def pair_rows(v):
            vv = jnp.concatenate([jnp.where(even, v, 0.0), jnp.where(even, 0.0, v)], axis=0)
            hi, lo = _split2(vv)
            return _dot_nt(pairsel_ref[...], hi) + _dot_nt(pairsel_ref[...], lo)
        acum_t = pair_rows(acum)
        dt_t = pair_rows(dt)

        def pair_cols(v, h0):
            return jnp.where(left, v[:, h0:h0 + 1], v[:, h0 + 1:h0 + 2])

        pw = 2 * SSD_HEADDIM
        for g in range(SSD_GROUPS):
            bm = xbc_ref[rows, inner + g * SSD_STATE:inner + (g + 1) * SSD_STATE].astype(BF16)
            cm = xbc_ref[rows, inner + gs + g * SSD_STATE:inner + gs + (g + 1) * SSD_STATE].astype(BF16)
            cb2 = _dot_nt(cm, jnp.concatenate([bm, bm], axis=0))
            st = state_ref[g]
            c_st = _dot(cm, st.astype(BF16))
            xw_parts, decay_parts = [], []
            for pj in range(hpg // 2):
                j = g * (hpg // 2) + pj
                h0 = 2 * j
                a_col = pair_cols(acum, h0)
                e_col = jnp.exp(a_col)
                xp = xbc_ref[rows, j * pw:(j + 1) * pw]
                xw_parts.append((xp * pair_cols(w_end, h0)).astype(BF16))
                decay_parts.append(e_col[CHUNK - 1:CHUNK, :])
                seg = a_col - acum_t[j:j + 1, :]
                lmat = jnp.exp(jnp.where(causal2, seg, -jnp.inf))
                wgt = (cb2 * lmat * dt_t[j:j + 1, :]).astype(BF16)
                rhs = jnp.concatenate([jnp.where(left, xp, 0.0), jnp.where(left, 0.0, xp)], axis=0)
                y_diag = _dot(wgt, rhs.astype(BF16))
                y_off = e_col * c_st[:, pj * pw:(pj + 1) * pw]
                y_ref[rows, j * pw:(j + 1) * pw] = y_diag + y_off + dfull_ref[:, j * pw:(j + 1) * pw] * xp
            state_ref[g] = (jnp.concatenate(decay_parts, axis=1) * st
                            + _dot_tn(bm, jnp.concatenate(xw_parts, axis=1)))
        return carry

    lax.fori_loop(0, blk // CHUNK, chunk_body, 0, unroll=2)

    z = _dot(h, wz_ref[...])
    y = y_ref[...] * _silu(z)
    yn = jnp.concatenate([_rms(y[:, g * gw:(g + 1) * gw]) for g in range(SSD_GROUPS)], axis=1)
    yn = (yn * gn_ref[...]).astype(BF16)
    o_ref[0] = x + gt_ref[0] * _dot(yn, wout_ref[...])


def _ssd_call(x, sh, sc, gt, norm_g, w_in, conv_w, conv_b, dt_bias, a_log, d_skip, gnorm, w_out, *, blk):
    bsz, s, d = x.shape
    inner = w_out.shape[0]
    nheads = dt_bias.shape[0]
    gs = SSD_GROUPS * SSD_STATE
    cch = inner + 2 * gs
    assert nheads <= LANES and nheads % (2 * SSD_GROUPS) == 0 and 2 * SSD_HEADDIM == LANES
    w_z = w_in[:, :inner].astype(BF16)
    w_xbc = w_in[:, inner:inner + cch].astype(BF16)
    w_dt = jnp.pad(w_in[:, inner + cch:], ((0, 0), (0, LANES - nheads))).astype(BF16)
    dtb = jnp.pad(dt_bias, (0, LANES - nheads)).reshape(1, LANES)
    alog = jnp.pad(a_log, (0, LANES - nheads)).reshape(1, LANES)
    d_full = jnp.repeat(d_skip, SSD_HEADDIM).reshape(1, inner)
    npair_pad = max(SUBLANES, nheads // 2)
    pairsel = ((jnp.arange(LANES)[None, :] // 2) == jnp.arange(npair_pad)[:, None]).astype(BF16)
    row3 = pl.BlockSpec((1, 1, d), lambda b, l: (b, 0, 0))
    xspec = pl.BlockSpec((1, blk, d), lambda b, l: (b, l, 0))
    return pl.pallas_call(
        functools.partial(_ssd_kernel, inner=inner, nheads=nheads),
        grid=(bsz, s // blk),
        in_specs=[
            xspec, row3, row3, row3,
            _const_spec((1, d)),
            _const_spec((d, inner)),
            _const_spec((d, cch)),
            _const_spec((d, LANES)),
            _const_spec((SSD_CONV, cch)),
            _const_spec((1, cch)),
            _const_spec((1, LANES)),
            _const_spec((1, LANES)),
            _const_spec((1, inner)),
            _const_spec((1, inner)),
            _const_spec((inner, d)),
            _const_spec((npair_pad, LANES)),
        ],
        out_specs=xspec,
        out_shape=jax.ShapeDtypeStruct((bsz, s, d), F32),
        scratch_shapes=[
            pltpu.VMEM((blk + 2 * SUBLANES, cch), F32),
            pltpu.VMEM((blk, cch), F32),
            pltpu.VMEM((blk, LANES), F32),
            pltpu.VMEM((blk, inner), F32),
            pltpu.VMEM((SSD_GROUPS, SSD_STATE, inner // SSD_GROUPS), F32),
        ],
        compiler_params=pltpu.CompilerParams(
            dimension_semantics=("arbitrary", "arbitrary"), vmem_limit_bytes=VMEM_LIMIT),
        name="ssd_layer",
    )(x, sh, sc, gt, norm_g.reshape(1, d), w_z, w_xbc, w_dt, conv_w, conv_b.reshape(1, cch),
      dtb, alog, d_full, gnorm.reshape(1, inner), w_out.astype(BF16), pairsel)


def _route(logits_t, rb_ref):
    scores = _sigmoid(logits_t)
    sc = [scores[e:e + 1, :] for e in range(N_EXPERTS)]
    bi = [sc[e] + rb_ref[e:e + 1, :] for e in range(N_EXPERTS)]
    gscore = []
    for g in range(N_GROUPS):
        a, b, c, d = bi[4 * g:4 * g + 4]
        hi1, lo1 = jnp.maximum(a, b), jnp.minimum(a, b)
        hi2, lo2 = jnp.maximum(c, d), jnp.minimum(c, d)
        gscore.append(jnp.maximum(hi1, hi2) + jnp.maximum(jnp.minimum(hi1, hi2), jnp.maximum(lo1, lo2)))
    gates = []
    flags = []
    for g in range(N_GROUPS):
        gsel = None
        for o in range(N_GROUPS):
            if o == g:
                continue
            win = (gscore[g] >= gscore[o]) if o > g else (gscore[g] > gscore[o])
            gsel = win if gsel is None else jnp.logical_and(gsel, win)
        sel = []
        for i in range(EXPERTS_PER_GROUP):
            e = 4 * g + i
            rank = jnp.zeros_like(bi[e])
            for j in range(EXPERTS_PER_GROUP):
                if j == i:
                    continue
                o = 4 * g + j
                beats = (bi[o] >= bi[e]) if j < i else (bi[o] > bi[e])
                rank = rank + jnp.where(beats, 1.0, 0.0)
            chosen = jnp.logical_and(gsel, rank < 2.0)
            flags.append(jnp.where(chosen, 1.0, 0.0))
            sel.append(jnp.where(chosen, sc[e], 0.0))
        denom = sel[0] + sel[1] + sel[2] + sel[3]
        denom = jnp.where(gsel, denom, 1.0)
        gates.extend([s_ / denom for s_ in sel])
    return gates, flags


def _router_kernel(x_ref, sh_ref, sc_ref, nm_ref, rwt_ref, rb_ref, upper_ref,
                   h_ref, slot_ref, col_ref, cnt_ref):
    tm = x_ref.shape[0]
    hb = _ada_norm(x_ref[...], nm_ref[...], sh_ref[0], sc_ref[0]).astype(BF16)
    h_ref[...] = hb
    gates, flags = _route(_dot_nt(rwt_ref[...], hb), rb_ref)
    big = float(N_EXPERTS)
    e_lo = functools.reduce(jnp.minimum, [jnp.where(flags[e] > 0.0, float(e), big) for e in range(N_EXPERTS)])
    e_hi = functools.reduce(jnp.maximum, [jnp.where(flags[e] > 0.0, float(e), -1.0) for e in range(N_EXPERTS)])
    flag_mat = jnp.concatenate(flags, axis=0)
    prefix = _dot(flag_mat.astype(BF16), upper_ref[...])
    counts = jnp.sum(flag_mat, axis=1, keepdims=True)
    slot_of = []
    before = jnp.zeros((1, 1), F32)
    for e in range(N_EXPERTS):
        slot_of.append(prefix[e:e + 1, :] + before)
        before = before + jnp.floor((counts[e:e + 1, :] + (SUBLANES - 1)) * (1.0 / SUBLANES)) * SUBLANES

    def pick(eid, rows):
        return functools.reduce(
            jnp.add, [jnp.where(eid == float(e), rows[e], 0.0) for e in range(N_EXPERTS)])

    zero = jnp.zeros((1, tm), F32)
    s_lo, s_hi = pick(e_lo, slot_of), pick(e_hi, slot_of)
    slot_ref[0] = jnp.concatenate([s_lo, s_hi] + [zero] * (SUBLANES - 2), axis=0).astype(jnp.int32)
    rows = jnp.concatenate([s_lo, s_hi, pick(e_lo, gates), pick(e_hi, gates)] + [zero] * (N_EXPERTS - 4), axis=0)
    col_ref[...] = rows.T
    cnt_ref[0] = jnp.broadcast_to(counts, (N_EXPERTS, LANES))


def _router_call(x2d, sh, sc, norm_g, router_w, router_b, *, tm, tiles_per_batch):
    t, d = x2d.shape
    nt = t // tm
    row3 = pl.BlockSpec((1, 1, d), lambda i: (i // tiles_per_batch, 0, 0))
    rb = jnp.broadcast_to(router_b.reshape(N_EXPERTS, 1), (N_EXPERTS, tm))
    upper = (jnp.arange(tm)[:, None] < jnp.arange(tm)[None, :]).astype(BF16)
    return pl.pallas_call(
        _router_kernel,
        grid=(nt,),
        in_specs=[
            pl.BlockSpec((tm, d), lambda i: (i, 0)), row3, row3,
            _const_spec((1, d)),
            _const_spec((N_EXPERTS, d)),
            _const_spec((N_EXPERTS, tm)),
            _const_spec((tm, tm)),
        ],
        out_specs=[
            pl.BlockSpec((tm, d), lambda i: (i, 0)),
            pl.BlockSpec((1, SUBLANES, tm), lambda i: (i, 0, 0)),
            pl.BlockSpec((tm, N_EXPERTS), lambda i: (i, 0)),
            pl.BlockSpec((1, N_EXPERTS, LANES), lambda i: (i, 0, 0)),
        ],
        out_shape=[
            jax.ShapeDtypeStruct((t, d), BF16),
            jax.ShapeDtypeStruct((nt, SUBLANES, tm), jnp.int32),
            jax.ShapeDtypeStruct((t, N_EXPERTS), F32),
            jax.ShapeDtypeStruct((nt, N_EXPERTS, LANES), F32),
        ],
        compiler_params=pltpu.CompilerParams(dimension_semantics=("arbitrary",), vmem_limit_bytes=VMEM_LIMIT),
        name="moe_router",
    )(x2d, sh, sc, norm_g.reshape(1, d), router_w.T.astype(BF16), rb, upper)


def _pack_pair(a, b):
    ua = lax.bitcast_convert_type(a, jnp.uint32)
    ub = lax.bitcast_convert_type(b, jnp.uint32)
    return (ua >> 16) | (ub & jnp.uint32(0xFFFF0000))


def _unpack_pair(w):
    a = lax.bitcast_convert_type(w << 16, F32)
    b = lax.bitcast_convert_type(w & jnp.uint32(0xFFFF0000), F32)
    return a.astype(BF16), b.astype(BF16)


def _bf16_exact(x):
    return x.astype(BF16).astype(F32)


def _slot_rows(tm):
    return 2 * tm + N_EXPERTS * SUBLANES


def _run_copies(n_rows, max_rows, make_copy, op):
    for b in reversed(range(SUBLANES.bit_length() - 1, max_rows.bit_length())):
        size = 1 << b
        offset = pl.multiple_of((n_rows >> (b + 1)) << (b + 1), SUBLANES)

        @pl.when((n_rows & size) != 0)
        def _(size=size, offset=offset):
            op(make_copy(offset, size))


def _tile_runs_start(tile, tm, len_ref, lbase_ref, toff_ref, local_ref, global_ref, sem, to_global):
    for e in range(N_EXPERTS):
        idx = tile * N_EXPERTS + e
        lo = lbase_ref[idx]
        go = toff_ref[idx]

        def make_copy(offset, size, lo=lo, go=go):
            loc = local_ref.at[pl.ds(pl.multiple_of(lo + offset, SUBLANES), size)]
            glb = global_ref.at[pl.ds(pl.multiple_of(go + offset, SUBLANES), size)]
            return pltpu.make_async_copy(loc, glb, sem) if to_global else pltpu.make_async_copy(glb, loc, sem)

        _run_copies(len_ref[idx], tm, make_copy, _start)


def _tile_runs_wait(tile, tm, rows_ref, local_ref, global_ref, sem, to_global):
    def make_copy(offset, size):
        del offset
        loc = local_ref.at[pl.ds(0, size)]
        glb = global_ref.at[pl.ds(0, size)]
        return pltpu.make_async_copy(loc, glb, sem) if to_global else pltpu.make_async_copy(glb, loc, sem)

    _run_copies(rows_ref[tile], _slot_rows(tm), make_copy, _wait)


def _start(copy):
    copy.start()


def _wait(copy):
    copy.wait()


def _dispatch_kernel(cnt_ref, lbase_ref, toff_ref, rows_ref, padstart_ref, padlen_ref, nact_ref,
                     slot_ref, h_ref, xs_ref, buf_ref, zero_ref, sem, zsem, *, rt):
    i = pl.program_id(0)
    last = pl.num_programs(0) - 1
    cur = i % 2
    tm = h_ref.shape[0]
    half = h_ref.shape[1] // 2

    nslot = _slot_rows(tm)

    def wait_runs(tile, which):
        _tile_runs_wait(tile, tm, rows_ref, buf_ref.at[which], xs_ref, sem.at[which], True)

    @pl.when(i > 0)
    def _():
        wait_runs(i - 1, 1 - cur)

    sl = slot_ref[0]
    srow = lax.broadcasted_iota(jnp.int32, (nslot, tm), 0)
    perm = jnp.where(srow == sl[0:1, :], 1.0, jnp.where(srow == sl[1:2, :], 1.0, 0.0)).astype(BF16)
    rows = _dot(perm, h_ref[...])
    buf_ref[cur] = _pack_pair(rows[:, :half], rows[:, half:])
    _tile_runs_start(i, tm, cnt_ref, lbase_ref, toff_ref, buf_ref.at[cur], xs_ref, sem.at[cur], True)

    @pl.when(i == last)
    def _():
        wait_runs(i, cur)
        zero_ref[...] = jnp.zeros_like(zero_ref)
        for op in (_start, _wait):
            for e in range(N_EXPERTS):
                def make_copy(offset, size, e=e):
                    dst = xs_ref.at[pl.ds(pl.multiple_of(padstart_ref[e] + offset, SUBLANES), size)]
                    return pltpu.make_async_copy(zero_ref.at[pl.ds(0, size)], dst, zsem)
                _run_copies(padlen_ref[e], rt - 1, make_copy, op)

        def zero_tile(j, carry):
            copy = pltpu.make_async_copy(zero_ref, xs_ref.at[pl.ds(pl.multiple_of(j * rt, rt), rt)], zsem)
            copy.start()
            copy.wait()
            return carry

        lax.fori_loop(nact_ref[0], xs_ref.shape[0] // rt, zero_tile, 0)


def _dispatch_call(seg, slot_rows, h2d, n_rows, *, rt):
    t, d = h2d.shape
    nt, _, tm = slot_rows.shape
    grid_spec = pltpu.PrefetchScalarGridSpec(
        num_scalar_prefetch=7,
        grid=(nt,),
        in_specs=[
            pl.BlockSpec((1, SUBLANES, tm), lambda i, *_: (i, 0, 0)),
            pl.BlockSpec((tm, d), lambda i, *_: (i, 0)),
        ],
        out_specs=pl.BlockSpec(memory_space=pl.ANY),
        scratch_shapes=[
            pltpu.VMEM((2, _slot_rows(tm), d // 2), jnp.uint32),
            pltpu.VMEM((rt, d // 2), jnp.uint32),
            pltpu.SemaphoreType.DMA((2,)),
            pltpu.SemaphoreType.DMA,
        ],
    )
    return pl.pallas_call(
        functools.partial(_dispatch_kernel, rt=rt),
        grid_spec=grid_spec,
        out_shape=jax.ShapeDtypeStruct((n_rows, d // 2), jnp.uint32),
        compiler_params=pltpu.CompilerParams(dimension_semantics=("arbitrary",), vmem_limit_bytes=VMEM_LIMIT),
        name="moe_dispatch",
    )(seg["cnt"], seg["lbase"], seg["toff"], seg["rows"], seg["padstart"], seg["padlen"], seg["nact"],
      slot_rows, h2d)


def _expert_kernel(te_ref, nact_ref, xs_ref, wg_ref, wu_ref, wd_ref, o_ref, wgb_ref, wub_ref, wdb_ref):
    j = pl.program_id(0)
    active = j < nact_ref[0]
    half = xs_ref.shape[1]

    @pl.when(jnp.logical_and(active, jnp.logical_or(j == 0, te_ref[j] != te_ref[jnp.maximum(j - 1, 0)])))
    def _():
        wgb_ref[...] = wg_ref[0, 0].astype(BF16)
        wub_ref[...] = wu_ref[0, 0].astype(BF16)
        wdb_ref[...] = wd_ref[0, 0].astype(BF16)

    @pl.when(active)
    def _():
        xa, xb = _unpack_pair(xs_ref[...])
        gate = _dot(xa, wgb_ref[0:half, :]) + _dot(xb, wgb_ref[half:, :])
        up = _dot(xa, wub_ref[0:half, :]) + _dot(xb, wub_ref[half:, :])
        he = (_silu(gate) * up).astype(BF16)
        y = _dot(he, wdb_ref[...])
        o_ref[...] = _pack_pair(_bf16_exact(y[:, :half]), _bf16_exact(y[:, half:]))

    @pl.when(jnp.logical_not(active))
    def _():
        o_ref[...] = jnp.zeros_like(o_ref)


def _expert_call(tile_expert, nact, xs, w_gate, w_up, w_down, *, layer, rt):
    n_rows, half = xs.shape
    _, _, d, de = w_gate.shape

    def row_map(j, te, na):
        return (jnp.minimum(j, na[0] - 1), 0)

    grid_spec = pltpu.PrefetchScalarGridSpec(
        num_scalar_prefetch=2,
        grid=(n_rows // rt,),
        in_specs=[
            pl.BlockSpec((rt, half), row_map),
            pl.BlockSpec((1, 1, d, de), lambda j, te, na: (layer, te[j], 0, 0)),
            pl.BlockSpec((1, 1, d, de), lambda j, te, na: (layer, te[j], 0, 0)),
            pl.BlockSpec((1, 1, de, d), lambda j, te, na: (layer, te[j], 0, 0)),
        ],
        out_specs=pl.BlockSpec((rt, half), lambda j, te, na: (j, 0)),
        scratch_shapes=[pltpu.VMEM((d, de), BF16), pltpu.VMEM((d, de), BF16), pltpu.VMEM((de, d), BF16)],
    )
    return pl.pallas_call(
        _expert_kernel,
        grid_spec=grid_spec,
        out_shape=jax.ShapeDtypeStruct((n_rows, half), jnp.uint32),
        compiler_params=pltpu.CompilerParams(dimension_semantics=("arbitrary",), vmem_limit_bytes=VMEM_LIMIT),
        name="moe_experts",
    )(tile_expert, nact, xs, w_gate, w_up, w_down)


def _combine_kernel(cnt_ref, lbase_ref, toff_ref, rows_ref, x_ref, gt_ref, col_ref, fin_ref, ys_ref, o_ref,
                    buf_ref, sem, *, final):
    i = pl.program_id(0)
    n = pl.num_programs(0)
    cur = i % 2
    tm = x_ref.shape[0]

    nslot = _slot_rows(tm)

    def start_runs(tile, which):
        _tile_runs_start(tile, tm, cnt_ref, lbase_ref, toff_ref, buf_ref.at[which], ys_ref, sem.at[which], False)

    @pl.when(i == 0)
    def _():
        buf_ref[...] = jnp.zeros_like(buf_ref)
        start_runs(i, cur)

    @pl.when(i + 1 < n)
    def _():
        start_runs(i + 1, 1 - cur)

    _tile_runs_wait(i, tm, rows_ref, buf_ref.at[cur], ys_ref, sem.at[cur], False)
    ya, yb = _unpack_pair(buf_ref[cur])
    col = col_ref[...]
    lane = lax.broadcasted_iota(jnp.int32, (tm, nslot), 1).astype(F32)
    unsort = jnp.where(lane == col[:, 0:1], col[:, 2:3],
                       jnp.where(lane == col[:, 1:2], col[:, 3:4], 0.0)).astype(BF16)
    moe = jnp.concatenate([_dot(unsort, ya), _dot(unsort, yb)], axis=1)
    y = x_ref[...] + gt_ref[0] * moe
    if final:
        y = _rms(y) * fin_ref[...]
    o_ref[...] = y


def _combine_call(seg, x2d, gt, cols, norm_final, ys, *, tm, tiles_per_batch, final):
    t, d = x2d.shape
    nt = t // tm
    grid_spec = pltpu.PrefetchScalarGridSpec(
        num_scalar_prefetch=4,
        grid=(nt,),
        in_specs=[
            pl.BlockSpec((tm, d), lambda i, *_: (i, 0)),
            pl.BlockSpec((1, 1, d), lambda i, *_: (i // tiles_per_batch, 0, 0)),
            pl.BlockSpec((tm, N_EXPERTS), lambda i, *_: (i, 0)),
            pl.BlockSpec((1, d), lambda i, *_: (0, 0)),
            pl.BlockSpec(memory_space=pl.ANY),
        ],
        out_specs=pl.BlockSpec((tm, d), lambda i, *_: (i, 0)),
        scratch_shapes=[pltpu.VMEM((2, _slot_rows(tm), d // 2), jnp.uint32), pltpu.SemaphoreType.DMA((2,))],
    )
    return pl.pallas_call(
        functools.partial(_combine_kernel, final=final),
        grid_spec=grid_spec,
        out_shape=jax.ShapeDtypeStruct((t, d), F32),
        compiler_params=pltpu.CompilerParams(dimension_semantics=("arbitrary",), vmem_limit_bytes=VMEM_LIMIT),
        name="moe_combine",
    )(seg["cnt"], seg["lbase"], seg["toff"], seg["rows"], x2d, gt, cols, norm_final.reshape(1, d), ys)


def _moe_layer(x, sh, sc, gt, norm_g, router_w, router_b, w_gate, w_up, w_down, norm_final, *,
               layer, tm, rt, final):
    bsz, s, d = x.shape
    t = bsz * s
    x2d = x.reshape(t, d)
    tiles_per_batch = s // tm
    h2d, slot_rows, cols, cnt = _router_call(x2d, sh, sc, norm_g, router_w, router_b, tm=tm,
                                             tiles_per_batch=tiles_per_batch)
    cnt = cnt[:, :, 0].astype(jnp.int32)
    run = ((cnt + SUBLANES - 1) // SUBLANES) * SUBLANES
    total = jnp.sum(run, axis=0)
    padded = ((total + rt - 1) // rt) * rt
    ends = jnp.cumsum(padded)
    base = ends - padded
    nt = t // tm
    n_rows = -(-(2 * t + nt * N_EXPERTS * SUBLANES) // rt) * rt + N_EXPERTS * rt
    starts = jnp.arange(n_rows // rt, dtype=jnp.int32) * rt
    tile_expert = jnp.minimum(jnp.sum(starts[:, None] >= ends[None, :], axis=1), N_EXPERTS - 1).astype(jnp.int32)
    nact = (ends[-1:] // rt).astype(jnp.int32)
    seg = {
        "cnt": run.reshape(-1),
        "lbase": (jnp.cumsum(run, axis=1) - run).reshape(-1),
        "toff": (base[None, :] + jnp.cumsum(run, axis=0) - run).reshape(-1),
        "rows": jnp.sum(run, axis=1),
        "padstart": base + total,
        "padlen": padded - total,
        "nact": nact,
    }
    seg = {k: v.astype(jnp.int32) for k, v in seg.items()}
    xs = _dispatch_call(seg, slot_rows, h2d, n_rows, rt=rt)
    ys = _expert_call(tile_expert, nact, xs, w_gate, w_up, w_down, layer=layer, rt=rt)
    out = _combine_call(seg, x2d, gt, cols, norm_final, ys, tm=tm, tiles_per_batch=tiles_per_batch, final=final)
    return out.reshape(bsz, s, d)


def kernel(x, c, ada_w, ada_b, norm_mix, norm_ffn, norm_final, gla_w_in, gla_w_gate2, gla_b_gate2, gla_norm,
           gla_w_out, ssd_w_in, ssd_conv_w, ssd_conv_b, ssd_dt_bias, ssd_a_log, ssd_d, ssd_norm, ssd_w_out,
           router_w, router_b, moe_w_gate, moe_w_up, moe_w_down):
    depth = ada_w.shape[0]
    bsz, s, d = x.shape
    mod = _ada_call(c, ada_w, ada_b).reshape(depth, bsz, 6, 1, d)
    n_mixers = 2
    for i in range(depth):
        sh1, sc1, g1, sh2, sc2, g2 = (mod[i, :, t] for t in range(6))
        j = i // n_mixers
        if i % n_mixers == 0:
            x = _gla_call(x, sh1, sc1, g1, norm_mix[i], gla_w_in[j], gla_w_gate2[j], gla_b_gate2[j],
                          gla_norm[j], gla_w_out[j], blk=min(512, s))
        else:
            x = _ssd_call(x, sh1, sc1, g1, norm_mix[i], ssd_w_in[j], ssd_conv_w[j], ssd_conv_b[j],
                          ssd_dt_bias[j], ssd_a_log[j], ssd_d[j], ssd_norm[j], ssd_w_out[j], blk=min(256, s))
        x = _moe_layer(x, sh2, sc2, g2, norm_ffn[i], router_w, router_b, moe_w_gate, moe_w_up, moe_w_down,
                       norm_final, layer=i, tm=min(256, s), rt=512, final=(i == depth - 1))
    return x
```

```python
import functools

import jax
import jax.numpy as jnp
from jax import lax
from jax.experimental import pallas as pl
from jax.experimental.pallas import tpu as pltpu

F32 = jnp.float32
BF16 = jnp.bfloat16

EPS = 1e-6
CHUNK = 64

GLA_HEADS = 4
GLA_GATE_RANK = 16
GLA_GATE_TAU = 16.0
GLA_SAFE_DECAY = 40.0

SSD_HEADDIM = 64
SSD_GROUPS = 4
SSD_STATE = 128
SSD_CONV = 4

N_EXPERTS = 16
N_GROUPS = 4
EXPERTS_PER_GROUP = N_EXPERTS // N_GROUPS

LANES = 128
SUBLANES = 8
VMEM_LIMIT = 56 * 1024 * 1024


def _dot(a, b):
    return jnp.dot(a, b, preferred_element_type=F32)


def _dot_nt(a, b):
    return lax.dot_general(a, b, (((1,), (1,)), ((), ())), preferred_element_type=F32)


def _dot_tn(a, b):
    return lax.dot_general(a, b, (((0,), (0,)), ((), ())), preferred_element_type=F32)


def _split2(x):
    hi = x.astype(BF16)
    lo = (x - hi.astype(F32)).astype(BF16)
    return hi, lo


def _dot_hl(a_bf16, x):
    hi, lo = _split2(x)
    return _dot(a_bf16, hi) + _dot(a_bf16, lo)


def _sigmoid(x):
    return 1.0 / (1.0 + jnp.exp(-x))


def _silu(x):
    return x * (0.5 * jnp.tanh(0.5 * x) + 0.5)


def _softplus(x):
    return jnp.maximum(x, 0.0) + jnp.log(1.0 + jnp.exp(-jnp.abs(x)))


def _log_sigmoid(x):
    return jnp.minimum(x, 0.0) - jnp.log(1.0 + jnp.exp(-jnp.abs(x)))


def _rms(x):
    return x * lax.rsqrt(jnp.mean(x * x, axis=-1, keepdims=True) + EPS)


def _ada_norm(x, g, shift, scale):
    return _rms(x) * g * (1.0 + scale) + shift


def _const_spec(shape):
    nd = len(shape)
    return pl.BlockSpec(shape, lambda *_: (0,) * nd, pipeline_mode=pl.Buffered(1))


def _tri(n, dtype):
    r = lax.broadcasted_iota(jnp.int32, (n, n), 0)
    c = lax.broadcasted_iota(jnp.int32, (n, n), 1)
    return (c <= r).astype(dtype)


def _ada_kernel(c_ref, w_ref, b_ref, o_ref):
    cond = _silu(c_ref[...]).astype(BF16)
    o_ref[0] = _dot(cond, w_ref[0].astype(BF16)) + b_ref[0]


def _ada_call(c, ada_w, ada_b):
    depth, d, n = ada_w.shape
    bsz = c.shape[0]
    tn = 1536
    return pl.pallas_call(
        _ada_kernel,
        grid=(depth, n // tn),
        in_specs=[
            pl.BlockSpec((bsz, d), lambda i, j: (0, 0)),
            pl.BlockSpec((1, d, tn), lambda i, j: (i, 0, j)),
            pl.BlockSpec((1, 1, tn), lambda i, j: (i, 0, j)),
        ],
        out_specs=pl.BlockSpec((1, bsz, tn), lambda i, j: (i, 0, j)),
        out_shape=jax.ShapeDtypeStruct((depth, bsz, n), F32),
        compiler_params=pltpu.CompilerParams(
            dimension_semantics=("arbitrary", "arbitrary"), vmem_limit_bytes=VMEM_LIMIT),
        name="ada_mod",
    )(c, ada_w, ada_b.reshape(depth, 1, n))


def _gla_kernel(x_ref, sh_ref, sc_ref, gt_ref, nm_ref, wqkvr_ref, wglr_ref, wg2_ref, bg2_ref,
                hn_ref, wout_ref, o_ref, proj_ref, loga_ref, oacc_ref, state_ref, score_ref, b_ref, *, dk, dv):
    hk = dk // GLA_HEADS
    hv = dv // GLA_HEADS
    blk = x_ref.shape[1]

    @pl.when(pl.program_id(1) == 0)
    def _():
        state_ref[...] = jnp.zeros_like(state_ref)

    x = x_ref[0]
    h = _ada_norm(x, nm_ref[...], sh_ref[0], sc_ref[0]).astype(BF16)
    proj_ref[...] = _dot(h, wqkvr_ref[...])
    glr = _dot(h, wglr_ref[...]).astype(BF16)
    z = _dot(glr, wg2_ref[...]) + bg2_ref[...]
    loga_ref[...] = _log_sigmoid(z) * (1.0 / GLA_GATE_TAU)

    tri = _tri(CHUNK, BF16)
    row = lax.broadcasted_iota(jnp.int32, (CHUNK, CHUNK), 0)
    col = lax.broadcasted_iota(jnp.int32, (CHUNK, CHUNK), 1)
    causal = col <= row
    qscale = hk ** -0.5

    def chunk_body(c, carry):
        r0 = pl.multiple_of(c * CHUNK, CHUNK)
        rows = pl.ds(r0, CHUNK)
        b = _dot_hl(tri, loga_ref[rows, :])
        unsafe = jnp.max(-b[CHUNK - 1:CHUNK, :]) > GLA_SAFE_DECAY
        qts = []
        for hd in range(GLA_HEADS):
            bh = b[:, hd * hk:(hd + 1) * hk]
            q = proj_ref[rows, hd * hk:(hd + 1) * hk] * qscale
            k = proj_ref[rows, dk + hd * hk:dk + (hd + 1) * hk]
            qt = (q * jnp.exp(bh)).astype(BF16)
            kt = (k * jnp.exp(-bh)).astype(BF16)
            qts.append(qt)
            score_ref[hd] = jnp.where(causal, _dot_nt(qt, kt), 0.0)

        @pl.when(unsafe)
        def _():
            b_ref[...] = b
            for hd in range(GLA_HEADS):
                hcols = slice(hd * hk, (hd + 1) * hk)
                q = proj_ref[rows, hcols] * qscale
                bh = b_ref[:, hcols]
                score_ref[hd] = jnp.zeros((CHUNK, CHUNK), F32)

                def columns(g, carry2, hd=hd, hcols=hcols, q=q, bh=bh):
                    s0 = pl.multiple_of(g * SUBLANES, SUBLANES)
                    k8 = proj_ref[pl.ds(r0 + s0, SUBLANES), dk + hd * hk:dk + (hd + 1) * hk]
                    b8 = b_ref[pl.ds(s0, SUBLANES), hcols]
                    sc = score_ref[hd]
                    for u in range(SUBLANES):
                        s = s0 + u
                        decay = jnp.exp(jnp.where(row[:, 0:1] >= s, bh - b8[u:u + 1, :], -jnp.inf))
                        val = jnp.sum(q * k8[u:u + 1, :] * decay, axis=-1, keepdims=True)
                        sc = jnp.where(col == s, val, sc)
                    score_ref[hd] = sc
                    return carry2

                lax.fori_loop(0, CHUNK // SUBLANES, columns, 0)

        for hd in range(GLA_HEADS):
            bh = b[:, hd * hk:(hd + 1) * hk]
            k = proj_ref[rows, dk + hd * hk:dk + (hd + 1) * hk]
            v = proj_ref[rows, 2 * dk + hd * hv:2 * dk + (hd + 1) * hv].astype(BF16)
            b_last = bh[CHUNK - 1:CHUNK, :]
            k_end = (k * jnp.exp(b_last - bh)).astype(BF16)
            st = state_ref[hd]
            o = _dot(score_ref[hd].astype(BF16), v) + _dot_nt(qts[hd], st.astype(BF16))
            state_ref[hd] = jnp.exp(b_last) * st + _dot_tn(v, k_end)
            oacc_ref[rows, hd * hv:(hd + 1) * hv] = _rms(o) * hn_ref[...]
        return carry

    lax.fori_loop(0, blk // CHUNK, chunk_body, 0, unroll=True)

    r = proj_ref[:, 2 * dk + dv:2 * dk + 2 * dv]
    og = (oacc_ref[...] * _silu(r)).astype(BF16)
    o_ref[0] = x + gt_ref[0] * _dot(og, wout_ref[...])


def _gla_call(x, sh, sc, gt, norm_g, w_in, w_gate2, b_gate2, head_norm, w_out, *, blk):
    bsz, s, d = x.shape
    dk = w_gate2.shape[1]
    dv = w_out.shape[0]
    nq = 2 * dk + 2 * dv
    w_qkvr = w_in[:, :nq].astype(BF16)
    w_glr = jnp.pad(w_in[:, nq:], ((0, 0), (0, LANES - GLA_GATE_RANK))).astype(BF16)
    w_g2 = jnp.pad(w_gate2, ((0, LANES - GLA_GATE_RANK), (0, 0))).astype(BF16)
    row3 = pl.BlockSpec((1, 1, d), lambda b, l: (b, 0, 0))
    xspec = pl.BlockSpec((1, blk, d), lambda b, l: (b, l, 0))
    return pl.pallas_call(
        functools.partial(_gla_kernel, dk=dk, dv=dv),
        grid=(bsz, s // blk),
        in_specs=[
            xspec, row3, row3, row3,
            _const_spec((1, d)),
            _const_spec((d, nq)),
            _const_spec((d, LANES)),
            _const_spec((LANES, dk)),
            _const_spec((1, dk)),
            _const_spec((1, dv // GLA_HEADS)),
            _const_spec((dv, d)),
        ],
        out_specs=xspec,
        out_shape=jax.ShapeDtypeStruct((bsz, s, d), F32),
        scratch_shapes=[
            pltpu.VMEM((blk, nq), F32),
            pltpu.VMEM((blk, dk), F32),
            pltpu.VMEM((blk, dv), F32),
            pltpu.VMEM((GLA_HEADS, dv // GLA_HEADS, dk // GLA_HEADS), F32),
            pltpu.VMEM((GLA_HEADS, CHUNK, CHUNK), F32),
            pltpu.VMEM((CHUNK, dk), F32),
        ],
        compiler_params=pltpu.CompilerParams(
            dimension_semantics=("arbitrary", "arbitrary"), vmem_limit_bytes=VMEM_LIMIT),
        name="gla_layer",
    )(x, sh, sc, gt, norm_g.reshape(1, d), w_qkvr, w_glr, w_g2, b_gate2.reshape(1, dk),
      head_norm.reshape(1, -1), w_out.astype(BF16))


def _ssd_kernel(x_ref, sh_ref, sc_ref, gt_ref, nm_ref, wz_ref, wxbc_ref, wdt_ref, cw_ref, cb_ref,
                dtb_ref, alog_ref, dfull_ref, gn_ref, wout_ref, pairsel_ref,
                o_ref, pad_ref, xbc_ref, dt_ref, y_ref, state_ref, *, inner, nheads):
    blk = x_ref.shape[1]
    gs = SSD_GROUPS * SSD_STATE
    gw = inner // SSD_GROUPS
    first = pl.program_id(1) == 0

    @pl.when(first)
    def _():
        state_ref[...] = jnp.zeros_like(state_ref)
        pad_ref[0:SUBLANES, :] = jnp.zeros((SUBLANES, pad_ref.shape[1]), F32)

    x = x_ref[0]
    h = _ada_norm(x, nm_ref[...], sh_ref[0], sc_ref[0]).astype(BF16)

    pad_ref[SUBLANES:SUBLANES + blk, :] = _dot(h, wxbc_ref[...])
    conv = cb_ref[...] + cw_ref[SSD_CONV - 1:SSD_CONV, :] * pad_ref[SUBLANES:SUBLANES + blk, :]
    padded = pad_ref[0:SUBLANES + blk, :]
    for kk in range(SSD_CONV - 1):
        conv = conv + cw_ref[kk:kk + 1, :] * pltpu.roll(padded, SSD_CONV - 1 - kk, 0)[SUBLANES:SUBLANES + blk, :]
    xbc_ref[...] = _silu(conv)
    pad_ref[0:SUBLANES, :] = pad_ref[blk:blk + SUBLANES, :]

    dt_ref[...] = _softplus(_dot(h, wdt_ref[...]) + dtb_ref[...])
    a_neg = -jnp.exp(alog_ref[...])
    lane = lax.broadcasted_iota(jnp.int32, (1, LANES), 1)
    a_neg = jnp.where(lane < nheads, a_neg, 0.0)

    tri = _tri(CHUNK, BF16)
    row2 = lax.broadcasted_iota(jnp.int32, (CHUNK, 2 * CHUNK), 0)
    col2 = lax.broadcasted_iota(jnp.int32, (CHUNK, 2 * CHUNK), 1)
    left = col2 < CHUNK
    causal2 = jnp.where(left, col2, col2 - CHUNK) <= row2
    head_lane = lax.broadcasted_iota(jnp.int32, (CHUNK, LANES), 1)
    even = (head_lane % 2) == 0
    npairs = nheads // 2
    hpg = nheads // SSD_GROUPS

    def chunk_body(c, carry):
        r0 = pl.multiple_of(c * CHUNK, CHUNK)
        rows = pl.ds(r0, CHUNK)
        dt = dt_ref[rows, :]
        acum = _dot_hl(tri, dt * a_neg)
        a_last = acum[CHUNK - 1:CHUNK, :]
        w_end = dt * jnp.exp(a_last - acum)
        def pair_rows(v):
            vv = jnp.concatenate([jnp.where(even, v, 0.0), jnp.where(even, 0.0, v)], axis=0)
            hi, lo = _split2(vv)
            return _dot_nt(pairsel_ref[...], hi) + _dot_nt(pairsel_ref[...], lo)
        acum_t = pair_rows(acum)
        dt_t = pair_rows(dt)

        def pair_cols(v, h0):
            return jnp.where(left, v[:, h0:h0 + 1], v[:, h0 + 1:h0 + 2])

        pw = 2 * SSD_HEADDIM
        for g in range(SSD_GROUPS):
            bm = xbc_ref[rows, inner + g * SSD_STATE:inner + (g + 1) * SSD_STATE].astype(BF16)
            cm = xbc_ref[rows, inner + gs + g * SSD_STATE:inner + gs + (g + 1) * SSD_STATE].astype(BF16)
            cb2 = _dot_nt(cm, jnp.concatenate([bm, bm], axis=0))
            st = state_ref[g]
            c_st = _dot(cm, st.astype(BF16))
            xw_parts, decay_parts = [], []
            for pj in range(hpg // 2):
                j = g * (hpg // 2) + pj
                h0 = 2 * j
                a_col = pair_cols(acum, h0)
                e_col = jnp.exp(a_col)
                xp = xbc_ref[rows, j * pw:(j + 1) * pw]
                xw_parts.append((xp * pair_cols(w_end, h0)).astype(BF16))
                decay_parts.append(e_col[CHUNK - 1:CHUNK, :])
                seg = a_col - acum_t[j:j + 1, :]
                lmat = jnp.exp(jnp.where(causal2, seg, -jnp.inf))
                wgt = (cb2 * lmat * dt_t[j:j + 1, :]).astype(BF16)
                rhs = jnp.concatenate([jnp.where(left, xp, 0.0), jnp.where(left, 0.0, xp)], axis=0)
                y_diag = _dot(wgt, rhs.astype(BF16))
                y_off = e_col * c_st[:, pj * pw:(pj + 1) * pw]
                y_ref[rows, j * pw:(j + 1) * pw] = y_diag + y_off + dfull_ref[:, j * pw:(j + 1) * pw] * xp
            state_ref[g] = (jnp.concatenate(decay_parts, axis=1) * st
                            + _dot_tn(bm, jnp.concatenate(xw_parts, axis=1)))
        return carry

    lax.fori_loop(0, blk // CHUNK, chunk_body, 0, unroll=True)

    z = _dot(h, wz_ref[...])
    y = y_ref[...] * _silu(z)
    yn = jnp.concatenate([_rms(y[:, g * gw:(g + 1) * gw]) for g in range(SSD_GROUPS)], axis=1)
    yn = (yn * gn_ref[...]).astype(BF16)
    o_ref[0] = x + gt_ref[0] * _dot(yn, wout_ref[...])


def _ssd_call(x, sh, sc, gt, norm_g, w_in, conv_w, conv_b, dt_bias, a_log, d_skip, gnorm, w_out, *, blk):
    bsz, s, d = x.shape
    inner = w_out.shape[0]
    nheads = dt_bias.shape[0]
    gs = SSD_GROUPS * SSD_STATE
    cch = inner + 2 * gs
    assert nheads <= LANES and nheads % (2 * SSD_GROUPS) == 0 and 2 * SSD_HEADDIM == LANES
    w_z = w_in[:, :inner].astype(BF16)
    w_xbc = w_in[:, inner:inner + cch].astype(BF16)
    w_dt = jnp.pad(w_in[:, inner + cch:], ((0, 0), (0, LANES - nheads))).astype(BF16)
    dtb = jnp.pad(dt_bias, (0, LANES - nheads)).reshape(1, LANES)
    alog = jnp.pad(a_log, (0, LANES - nheads)).reshape(1, LANES)
    d_full = jnp.repeat(d_skip, SSD_HEADDIM).reshape(1, inner)
    npair_pad = max(SUBLANES, nheads // 2)
    pairsel = ((jnp.arange(LANES)[None, :] // 2) == jnp.arange(npair_pad)[:, None]).astype(BF16)
    row3 = pl.BlockSpec((1, 1, d), lambda b, l: (b, 0, 0))
    xspec = pl.BlockSpec((1, blk, d), lambda b, l: (b, l, 0))
    return pl.pallas_call(
        functools.partial(_ssd_kernel, inner=inner, nheads=nheads),
        grid=(bsz, s // blk),
        in_specs=[
            xspec, row3, row3, row3,
            _const_spec((1, d)),
            _const_spec((d, inner)),
            _const_spec((d, cch)),
            _const_spec((d, LANES)),
            _const_spec((SSD_CONV, cch)),
            _const_spec((1, cch)),
            _const_spec((1, LANES)),
            _const_spec((1, LANES)),
            _const_spec((1, inner)),
            _const_spec((1, inner)),
            _const_spec((inner, d)),
            _const_spec((npair_pad, LANES)),
        ],
        out_specs=xspec,
        out_shape=jax.ShapeDtypeStruct((bsz, s, d), F32),
        scratch_shapes=[
            pltpu.VMEM((blk + 2 * SUBLANES, cch), F32),
            pltpu.VMEM((blk, cch), F32),
            pltpu.VMEM((blk, LANES), F32),
            pltpu.VMEM((blk, inner), F32),
            pltpu.VMEM((SSD_GROUPS, SSD_STATE, inner // SSD_GROUPS), F32),
        ],
        compiler_params=pltpu.CompilerParams(
            dimension_semantics=("arbitrary", "arbitrary"), vmem_limit_bytes=VMEM_LIMIT),
        name="ssd_layer",
    )(x, sh, sc, gt, norm_g.reshape(1, d), w_z, w_xbc, w_dt, conv_w, conv_b.reshape(1, cch),
      dtb, alog, d_full, gnorm.reshape(1, inner), w_out.astype(BF16), pairsel)


def _route(logits_t, rb_ref):
    scores = _sigmoid(logits_t)
    sc = [scores[e:e + 1, :] for e in range(N_EXPERTS)]
    bi = [sc[e] + rb_ref[e:e + 1, :] for e in range(N_EXPERTS)]
    gscore = []
    for g in range(N_GROUPS):
        a, b, c, d = bi[4 * g:4 * g + 4]
        hi1, lo1 = jnp.maximum(a, b), jnp.minimum(a, b)
        hi2, lo2 = jnp.maximum(c, d), jnp.minimum(c, d)
        gscore.append(jnp.maximum(hi1, hi2) + jnp.maximum(jnp.minimum(hi1, hi2), jnp.maximum(lo1, lo2)))
    gates = []
    flags = []
    for g in range(N_GROUPS):
        gsel = None
        for o in range(N_GROUPS):
            if o == g:
                continue
            win = (gscore[g] >= gscore[o]) if o > g else (gscore[g] > gscore[o])
            gsel = win if gsel is None else jnp.logical_and(gsel, win)
        sel = []
        for i in range(EXPERTS_PER_GROUP):
            e = 4 * g + i
            rank = jnp.zeros_like(bi[e])
            for j in range(EXPERTS_PER_GROUP):
                if j == i:
                    continue
                o = 4 * g + j
                beats = (bi[o] >= bi[e]) if j < i else (bi[o] > bi[e])
                rank = rank + jnp.where(beats, 1.0, 0.0)
            chosen = jnp.logical_and(gsel, rank < 2.0)
            flags.append(jnp.where(chosen, 1.0, 0.0))
            sel.append(jnp.where(chosen, sc[e], 0.0))
        denom = sel[0] + sel[1] + sel[2] + sel[3]
        denom = jnp.where(gsel, denom, 1.0)
        gates.extend([s_ / denom for s_ in sel])
    return gates, flags


def _router_kernel(x_ref, sh_ref, sc_ref, nm_ref, rwt_ref, rb_ref, upper_ref,
                   h_ref, slot_ref, col_ref, cnt_ref):
    tm = x_ref.shape[0]
    hb = _ada_norm(x_ref[...], nm_ref[...], sh_ref[0], sc_ref[0]).astype(BF16)
    h_ref[...] = hb
    gates, flags = _route(_dot_nt(rwt_ref[...], hb), rb_ref)
    big = float(N_EXPERTS)
    e_lo = functools.reduce(jnp.minimum, [jnp.where(flags[e] > 0.0, float(e), big) for e in range(N_EXPERTS)])
    e_hi = functools.reduce(jnp.maximum, [jnp.where(flags[e] > 0.0, float(e), -1.0) for e in range(N_EXPERTS)])
    flag_mat = jnp.concatenate(flags, axis=0)
    prefix = _dot(flag_mat.astype(BF16), upper_ref[...])
    counts = jnp.sum(flag_mat, axis=1, keepdims=True)
    slot_of = []
    before = jnp.zeros((1, 1), F32)
    for e in range(N_EXPERTS):
        slot_of.append(prefix[e:e + 1, :] + before)
        before = before + jnp.floor((counts[e:e + 1, :] + (SUBLANES - 1)) * (1.0 / SUBLANES)) * SUBLANES

    def pick(eid, rows):
        return functools.reduce(
            jnp.add, [jnp.where(eid == float(e), rows[e], 0.0) for e in range(N_EXPERTS)])

    zero = jnp.zeros((1, tm), F32)
    s_lo, s_hi = pick(e_lo, slot_of), pick(e_hi, slot_of)
    slot_ref[0] = jnp.concatenate([s_lo, s_hi] + [zero] * (SUBLANES - 2), axis=0).astype(jnp.int32)
    rows = jnp.concatenate([s_lo, s_hi, pick(e_lo, gates), pick(e_hi, gates)] + [zero] * (N_EXPERTS - 4), axis=0)
    col_ref[...] = rows.T
    cnt_ref[0] = jnp.broadcast_to(counts, (N_EXPERTS, LANES))


def _router_call(x2d, sh, sc, norm_g, router_w, router_b, *, tm, tiles_per_batch):
    t, d = x2d.shape
    nt = t // tm
    row3 = pl.BlockSpec((1, 1, d), lambda i: (i // tiles_per_batch, 0, 0))
    rb = jnp.broadcast_to(router_b.reshape(N_EXPERTS, 1), (N_EXPERTS, tm))
    upper = (jnp.arange(tm)[:, None] < jnp.arange(tm)[None, :]).astype(BF16)
    return pl.pallas_call(
        _router_kernel,
        grid=(nt,),
        in_specs=[
            pl.BlockSpec((tm, d), lambda i: (i, 0)), row3, row3,
            _const_spec((1, d)),
            _const_spec((N_EXPERTS, d)),
            _const_spec((N_EXPERTS, tm)),
            _const_spec((tm, tm)),
        ],
        out_specs=[
            pl.BlockSpec((tm, d), lambda i: (i, 0)),
            pl.BlockSpec((1, SUBLANES, tm), lambda i: (i, 0, 0)),
            pl.BlockSpec((tm, N_EXPERTS), lambda i: (i, 0)),
            pl.BlockSpec((1, N_EXPERTS, LANES), lambda i: (i, 0, 0)),
        ],
        out_shape=[
            jax.ShapeDtypeStruct((t, d), BF16),
            jax.ShapeDtypeStruct((nt, SUBLANES, tm), jnp.int32),
            jax.ShapeDtypeStruct((t, N_EXPERTS), F32),
            jax.ShapeDtypeStruct((nt, N_EXPERTS, LANES), F32),
        ],
        compiler_params=pltpu.CompilerParams(dimension_semantics=("arbitrary",), vmem_limit_bytes=VMEM_LIMIT),
        name="moe_router",
    )(x2d, sh, sc, norm_g.reshape(1, d), router_w.T.astype(BF16), rb, upper)


def _pack_pair(a, b):
    ua = lax.bitcast_convert_type(a, jnp.uint32)
    ub = lax.bitcast_convert_type(b, jnp.uint32)
    return (ua >> 16) | (ub & jnp.uint32(0xFFFF0000))


def _unpack_pair(w):
    a = lax.bitcast_convert_type(w << 16, F32)
    b = lax.bitcast_convert_type(w & jnp.uint32(0xFFFF0000), F32)
    return a.astype(BF16), b.astype(BF16)


def _bf16_exact(x):
    return x.astype(BF16).astype(F32)


def _slot_rows(tm):
    return 2 * tm + N_EXPERTS * SUBLANES


def _run_copies(n_rows, max_rows, make_copy, op):
    for b in reversed(range(SUBLANES.bit_length() - 1, max_rows.bit_length())):
        size = 1 << b
        offset = pl.multiple_of((n_rows >> (b + 1)) << (b + 1), SUBLANES)

        @pl.when((n_rows & size) != 0)
        def _(size=size, offset=offset):
            op(make_copy(offset, size))


def _tile_runs_start(tile, tm, len_ref, lbase_ref, toff_ref, local_ref, global_ref, sem, to_global):
    for e in range(N_EXPERTS):
        idx = tile * N_EXPERTS + e
        lo = lbase_ref[idx]
        go = toff_ref[idx]

        def make_copy(offset, size, lo=lo, go=go):
            loc = local_ref.at[pl.ds(pl.multiple_of(lo + offset, SUBLANES), size)]
            glb = global_ref.at[pl.ds(pl.multiple_of(go + offset, SUBLANES), size)]
            return pltpu.make_async_copy(loc, glb, sem) if to_global else pltpu.make_async_copy(glb, loc, sem)

        _run_copies(len_ref[idx], tm, make_copy, _start)


def _tile_runs_wait(tile, tm, rows_ref, local_ref, global_ref, sem, to_global):
    def make_copy(offset, size):
        del offset
        loc = local_ref.at[pl.ds(0, size)]
        glb = global_ref.at[pl.ds(0, size)]
        return pltpu.make_async_copy(loc, glb, sem) if to_global else pltpu.make_async_copy(glb, loc, sem)

    _run_copies(rows_ref[tile], _slot_rows(tm), make_copy, _wait)


def _start(copy):
    copy.start()


def _wait(copy):
    copy.wait()


def _dispatch_kernel(cnt_ref, lbase_ref, toff_ref, rows_ref, padstart_ref, padlen_ref, nact_ref,
                     slot_ref, h_ref, xs_ref, buf_ref, zero_ref, sem, zsem, *, rt):
    i = pl.program_id(0)
    last = pl.num_programs(0) - 1
    cur = i % 2
    tm = h_ref.shape[0]
    half = h_ref.shape[1] // 2

    nslot = _slot_rows(tm)

    def wait_runs(tile, which):
        _tile_runs_wait(tile, tm, rows_ref, buf_ref.at[which], xs_ref, sem.at[which], True)

    @pl.when(i > 0)
    def _():
        wait_runs(i - 1, 1 - cur)

    sl = slot_ref[0]
    srow = lax.broadcasted_iota(jnp.int32, (nslot, tm), 0)
    perm = jnp.where(srow == sl[0:1, :], 1.0, jnp.where(srow == sl[1:2, :], 1.0, 0.0)).astype(BF16)
    rows = _dot(perm, h_ref[...])
    buf_ref[cur] = _pack_pair(rows[:, :half], rows[:, half:])
    _tile_runs_start(i, tm, cnt_ref, lbase_ref, toff_ref, buf_ref.at[cur], xs_ref, sem.at[cur], True)

    @pl.when(i == last)
    def _():
        wait_runs(i, cur)
        zero_ref[...] = jnp.zeros_like(zero_ref)
        for op in (_start, _wait):
            for e in range(N_EXPERTS):
                def make_copy(offset, size, e=e):
                    dst = xs_ref.at[pl.ds(pl.multiple_of(padstart_ref[e] + offset, SUBLANES), size)]
                    return pltpu.make_async_copy(zero_ref.at[pl.ds(0, size)], dst, zsem)
                _run_copies(padlen_ref[e], rt - 1, make_copy, op)

        def zero_tile(j, carry):
            copy = pltpu.make_async_copy(zero_ref, xs_ref.at[pl.ds(pl.multiple_of(j * rt, rt), rt)], zsem)
            copy.start()
            copy.wait()
            return carry

        lax.fori_loop(nact_ref[0], xs_ref.shape[0] // rt, zero_tile, 0)


def _dispatch_call(seg, slot_rows, h2d, n_rows, *, rt):
    t, d = h2d.shape
    nt, _, tm = slot_rows.shape
    grid_spec = pltpu.PrefetchScalarGridSpec(
        num_scalar_prefetch=7,
        grid=(nt,),
        in_specs=[
            pl.BlockSpec((1, SUBLANES, tm), lambda i, *_: (i, 0, 0)),
            pl.BlockSpec((tm, d), lambda i, *_: (i, 0)),
        ],
        out_specs=pl.BlockSpec(memory_space=pl.ANY),
        scratch_shapes=[
            pltpu.VMEM((2, _slot_rows(tm), d // 2), jnp.uint32),
            pltpu.VMEM((rt, d // 2), jnp.uint32),
            pltpu.SemaphoreType.DMA((2,)),
            pltpu.SemaphoreType.DMA,
        ],
    )
    return pl.pallas_call(
        functools.partial(_dispatch_kernel, rt=rt),
        grid_spec=grid_spec,
        out_shape=jax.ShapeDtypeStruct((n_rows, d // 2), jnp.uint32),
        compiler_params=pltpu.CompilerParams(dimension_semantics=("arbitrary",), vmem_limit_bytes=VMEM_LIMIT),
        name="moe_dispatch",
    )(seg["cnt"], seg["lbase"], seg["toff"], seg["rows"], seg["padstart"], seg["padlen"], seg["nact"],
      slot_rows, h2d)


def _expert_kernel(te_ref, nact_ref, xs_ref, wg_ref, wu_ref, wd_ref, o_ref, wgb_ref, wub_ref, wdb_ref):
    j = pl.program_id(0)
    active = j < nact_ref[0]
    half = xs_ref.shape[1]

    @pl.when(jnp.logical_and(active, jnp.logical_or(j == 0, te_ref[j] != te_ref[jnp.maximum(j - 1, 0)])))
    def _():
        wgb_ref[...] = wg_ref[0, 0].astype(BF16)
        wub_ref[...] = wu_ref[0, 0].astype(BF16)
        wdb_ref[...] = wd_ref[0, 0].astype(BF16)

    @pl.when(active)
    def _():
        xa, xb = _unpack_pair(xs_ref[...])
        gate = _dot(xa, wgb_ref[0:half, :]) + _dot(xb, wgb_ref[half:, :])
        up = _dot(xa, wub_ref[0:half, :]) + _dot(xb, wub_ref[half:, :])
        he = (_silu(gate) * up).astype(BF16)
        y = _dot(he, wdb_ref[...])
        o_ref[...] = _pack_pair(_bf16_exact(y[:, :half]), _bf16_exact(y[:, half:]))

    @pl.when(jnp.logical_not(active))
    def _():
        o_ref[...] = jnp.zeros_like(o_ref)


def _expert_call(tile_expert, nact, xs, w_gate, w_up, w_down, *, layer, rt):
    n_rows, half = xs.shape
    _, _, d, de = w_gate.shape

    def row_map(j, te, na):
        return (jnp.minimum(j, na[0] - 1), 0)

    grid_spec = pltpu.PrefetchScalarGridSpec(
        num_scalar_prefetch=2,
        grid=(n_rows // rt,),
        in_specs=[
            pl.BlockSpec((rt, half), row_map),
            pl.BlockSpec((1, 1, d, de), lambda j, te, na: (layer, te[j], 0, 0)),
            pl.BlockSpec((1, 1, d, de), lambda j, te, na: (layer, te[j], 0, 0)),
            pl.BlockSpec((1, 1, de, d), lambda j, te, na: (layer, te[j], 0, 0)),
        ],
        out_specs=pl.BlockSpec((rt, half), lambda j, te, na: (j, 0)),
        scratch_shapes=[pltpu.VMEM((d, de), BF16), pltpu.VMEM((d, de), BF16), pltpu.VMEM((de, d), BF16)],
    )
    return pl.pallas_call(
        _expert_kernel,
        grid_spec=grid_spec,
        out_shape=jax.ShapeDtypeStruct((n_rows, half), jnp.uint32),
        compiler_params=pltpu.CompilerParams(dimension_semantics=("arbitrary",), vmem_limit_bytes=VMEM_LIMIT),
        name="moe_experts",
    )(tile_expert, nact, xs, w_gate, w_up, w_down)


def _combine_kernel(cnt_ref, lbase_ref, toff_ref, rows_ref, x_ref, gt_ref, col_ref, fin_ref, ys_ref, o_ref,
                    buf_ref, sem, *, final):
    i = pl.program_id(0)
    n = pl.num_programs(0)
    cur = i % 2
    tm = x_ref.shape[0]

    nslot = _slot_rows(tm)

    def start_runs(tile, which):
        _tile_runs_start(tile, tm, cnt_ref, lbase_ref, toff_ref, buf_ref.at[which], ys_ref, sem.at[which], False)

    @pl.when(i == 0)
    def _():
        buf_ref[...] = jnp.zeros_like(buf_ref)
        start_runs(i, cur)

    @pl.when(i + 1 < n)
    def _():
        start_runs(i + 1, 1 - cur)

    _tile_runs_wait(i, tm, rows_ref, buf_ref.at[cur], ys_ref, sem.at[cur], False)
    ya, yb = _unpack_pair(buf_ref[cur])
    col = col_ref[...]
    lane = lax.broadcasted_iota(jnp.int32, (tm, nslot), 1).astype(F32)
    unsort = jnp.where(lane == col[:, 0:1], col[:, 2:3],
                       jnp.where(lane == col[:, 1:2], col[:, 3:4], 0.0)).astype(BF16)
    moe = jnp.concatenate([_dot(unsort, ya), _dot(unsort, yb)], axis=1)
    y = x_ref[...] + gt_ref[0] * moe
    if final:
        y = _rms(y) * fin_ref[...]
    o_ref[...] = y


def _combine_call(seg, x2d, gt, cols, norm_final, ys, *, tm, tiles_per_batch, final):
    t, d = x2d.shape
    nt = t // tm
    grid_spec = pltpu.PrefetchScalarGridSpec(
        num_scalar_prefetch=4,
        grid=(nt,),
        in_specs=[
            pl.BlockSpec((tm, d), lambda i, *_: (i, 0)),
            pl.BlockSpec((1, 1, d), lambda i, *_: (i // tiles_per_batch, 0, 0)),
            pl.BlockSpec((tm, N_EXPERTS), lambda i, *_: (i, 0)),
            pl.BlockSpec((1, d), lambda i, *_: (0, 0)),
            pl.BlockSpec(memory_space=pl.ANY),
        ],
        out_specs=pl.BlockSpec((tm, d), lambda i, *_: (i, 0)),
        scratch_shapes=[pltpu.VMEM((2, _slot_rows(tm), d // 2), jnp.uint32), pltpu.SemaphoreType.DMA((2,))],
    )
    return pl.pallas_call(
        functools.partial(_combine_kernel, final=final),
        grid_spec=grid_spec,
        out_shape=jax.ShapeDtypeStruct((t, d), F32),
        compiler_params=pltpu.CompilerParams(dimension_semantics=("arbitrary",), vmem_limit_bytes=VMEM_LIMIT),
        name="moe_combine",
    )(seg["cnt"], seg["lbase"], seg["toff"], seg["rows"], x2d, gt, cols, norm_final.reshape(1, d), ys)


def _moe_layer(x, sh, sc, gt, norm_g, router_w, router_b, w_gate, w_up, w_down, norm_final, *,
               layer, tm, rt, final):
    bsz, s, d = x.shape
    t = bsz * s
    x2d = x.reshape(t, d)
    tiles_per_batch = s // tm
    h2d, slot_rows, cols, cnt = _router_call(x2d, sh, sc, norm_g, router_w, router_b, tm=tm,
                                             tiles_per_batch=tiles_per_batch)
    cnt = cnt[:, :, 0].astype(jnp.int32)
    run = ((cnt + SUBLANES - 1) // SUBLANES) * SUBLANES
    total = jnp.sum(run, axis=0)
    padded = ((total + rt - 1) // rt) * rt
    ends = jnp.cumsum(padded)
    base = ends - padded
    nt = t // tm
    n_rows = -(-(2 * t + nt * N_EXPERTS * SUBLANES) // rt) * rt + N_EXPERTS * rt
    starts = jnp.arange(n_rows // rt, dtype=jnp.int32) * rt
    tile_expert = jnp.minimum(jnp.sum(starts[:, None] >= ends[None, :], axis=1), N_EXPERTS - 1).astype(jnp.int32)
    nact = (ends[-1:] // rt).astype(jnp.int32)
    seg = {
        "cnt": run.reshape(-1),
        "lbase": (jnp.cumsum(run, axis=1) - run).reshape(-1),
        "toff": (base[None, :] + jnp.cumsum(run, axis=0) - run).reshape(-1),
        "rows": jnp.sum(run, axis=1),
        "padstart": base + total,
        "padlen": padded - total,
        "nact": nact,
    }
    seg = {k: v.astype(jnp.int32) for k, v in seg.items()}
    xs = _dispatch_call(seg, slot_rows, h2d, n_rows, rt=rt)
    ys = _expert_call(tile_expert, nact, xs, w_gate, w_up, w_down, layer=layer, rt=rt)
    out = _combine_call(seg, x2d, gt, cols, norm_final, ys, tm=tm, tiles_per_batch=tiles_per_batch, final=final)
    return out.reshape(bsz, s, d)


def kernel(x, c, ada_w, ada_b, norm_mix, norm_ffn, norm_final, gla_w_in, gla_w_gate2, gla_b_gate2, gla_norm,
           gla_w_out, ssd_w_in, ssd_conv_w, ssd_conv_b, ssd_dt_bias, ssd_a_log, ssd_d, ssd_norm, ssd_w_out,
           router_w, router_b, moe_w_gate, moe_w_up, moe_w_down):
    depth = ada_w.shape[0]
    bsz, s, d = x.shape
    mod = _ada_call(c, ada_w, ada_b).reshape(depth, bsz, 6, 1, d)
    n_mixers = 2
    for i in range(depth):
        sh1, sc1, g1, sh2, sc2, g2 = (mod[i, :, t] for t in range(6))
        j = i // n_mixers
        if i % n_mixers == 0:
            x = _gla_call(x, sh1, sc1, g1, norm_mix[i], gla_w_in[j], gla_w_gate2[j], gla_b_gate2[j],
                          gla_norm[j], gla_w_out[j], blk=min(512, s))
        else:
            x = _ssd_call(x, sh1, sc1, g1, norm_mix[i], ssd_w_in[j], ssd_conv_w[j], ssd_conv_b[j],
                          ssd_dt_bias[j], ssd_a_log[j], ssd_d[j], ssd_norm[j], ssd_w_out[j], blk=min(256, s))
        x = _moe_layer(x, sh2, sc2, g2, norm_ffn[i], router_w, router_b, moe_w_gate, moe_w_up, moe_w_down,
                       norm_final, layer=i, tm=min(256, s), rt=512, final=(i == depth - 1))
    return x
```

```python
import functools

import jax
import jax.numpy as jnp
from jax import lax
from jax.experimental import pallas as pl
from jax.experimental.pallas import tpu as pltpu

F32 = jnp.float32
BF16 = jnp.bfloat16

EPS = 1e-6
CHUNK = 64

GLA_HEADS = 4
GLA_GATE_RANK = 16
GLA_GATE_TAU = 16.0
GLA_SAFE_DECAY = 40.0

SSD_HEADDIM = 64
SSD_GROUPS = 4
SSD_STATE = 128
SSD_CONV = 4

N_EXPERTS = 16
N_GROUPS = 4
EXPERTS_PER_GROUP = N_EXPERTS // N_GROUPS

LANES = 128
SUBLANES = 8
VMEM_LIMIT = 56 * 1024 * 1024


def _dot(a, b):
    return jnp.dot(a, b, preferred_element_type=F32)


def _dot_nt(a, b):
    return lax.dot_general(a, b, (((1,), (1,)), ((), ())), preferred_element_type=F32)


def _dot_tn(a, b):
    return lax.dot_general(a, b, (((0,), (0,)), ((), ())), preferred_element_type=F32)


def _split2(x):
    hi = x.astype(BF16)
    lo = (x - hi.astype(F32)).astype(BF16)
    return hi, lo


def _dot_hl(a_bf16, x):
    hi, lo = _split2(x)
    return _dot(a_bf16, hi) + _dot(a_bf16, lo)


def _sigmoid(x):
    return 1.0 / (1.0 + jnp.exp(-x))


def _silu(x):
    return x * (0.5 * jnp.tanh(0.5 * x) + 0.5)


def _softplus(x):
    return jnp.maximum(x, 0.0) + jnp.log(1.0 + jnp.exp(-jnp.abs(x)))


def _log_sigmoid(x):
    return jnp.minimum(x, 0.0) - jnp.log(1.0 + jnp.exp(-jnp.abs(x)))


def _rms(x):
    return x * lax.rsqrt(jnp.mean(x * x, axis=-1, keepdims=True) + EPS)


def _ada_norm(x, g, shift, scale):
    return _rms(x) * g * (1.0 + scale) + shift


def _const_spec(shape):
    nd = len(shape)
    return pl.BlockSpec(shape, lambda *_: (0,) * nd, pipeline_mode=pl.Buffered(1))


def _tri(n, dtype):
    r = lax.broadcasted_iota(jnp.int32, (n, n), 0)
    c = lax.broadcasted_iota(jnp.int32, (n, n), 1)
    return (c <= r).astype(dtype)


def _ada_kernel(c_ref, w_ref, b_ref, o_ref):
    cond = _silu(c_ref[...]).astype(BF16)
    o_ref[0] = _dot(cond, w_ref[0].astype(BF16)) + b_ref[0]


def _ada_call(c, ada_w, ada_b):
    depth, d, n = ada_w.shape
    bsz = c.shape[0]
    tn = 1536
    return pl.pallas_call(
        _ada_kernel,
        grid=(depth, n // tn),
        in_specs=[
            pl.BlockSpec((bsz, d), lambda i, j: (0, 0)),
            pl.BlockSpec((1, d, tn), lambda i, j: (i, 0, j)),
            pl.BlockSpec((1, 1, tn), lambda i, j: (i, 0, j)),
        ],
        out_specs=pl.BlockSpec((1, bsz, tn), lambda i, j: (i, 0, j)),
        out_shape=jax.ShapeDtypeStruct((depth, bsz, n), F32),
        compiler_params=pltpu.CompilerParams(
            dimension_semantics=("arbitrary", "arbitrary"), vmem_limit_bytes=VMEM_LIMIT),
        name="ada_mod",
    )(c, ada_w, ada_b.reshape(depth, 1, n))


def _gla_kernel(x_ref, sh_ref, sc_ref, gt_ref, nm_ref, wqkvr_ref, wglr_ref, wg2_ref, bg2_ref,
                hn_ref, wout_ref, o_ref, proj_ref, loga_ref, oacc_ref, state_ref, b_ref, *, dk, dv):
    hk = dk // GLA_HEADS
    hv = dv // GLA_HEADS
    blk = x_ref.shape[1]

    @pl.when(pl.program_id(1) == 0)
    def _():
        state_ref[...] = jnp.zeros_like(state_ref)

    x = x_ref[0]
    h = _ada_norm(x, nm_ref[...], sh_ref[0], sc_ref[0]).astype(BF16)
    proj_ref[...] = _dot(h, wqkvr_ref[...])
    glr = _dot(h, wglr_ref[...]).astype(BF16)
    z = _dot(glr, wg2_ref[...]) + bg2_ref[...]
    loga_ref[...] = _log_sigmoid(z) * (1.0 / GLA_GATE_TAU)

    tri = _tri(CHUNK, BF16)
    row = lax.broadcasted_iota(jnp.int32, (CHUNK, CHUNK), 0)
    col = lax.broadcasted_iota(jnp.int32, (CHUNK, CHUNK), 1)
    causal = col <= row
    qscale = hk ** -0.5

    nchunks = blk // CHUNK
    for c in range(nchunks):
        b_ref[c * CHUNK:(c + 1) * CHUNK, :] = _dot_hl(tri, loga_ref[c * CHUNK:(c + 1) * CHUNK, :])

    def exact_scores(r0, rows, hd):
        hcols = slice(hd * hk, (hd + 1) * hk)
        q = proj_ref[rows, hcols] * qscale
        bh = b_ref[rows, hcols]

        def columns(g, sc):
            s0 = pl.multiple_of(g * SUBLANES, SUBLANES)
            k8 = proj_ref[pl.ds(r0 + s0, SUBLANES), dk + hd * hk:dk + (hd + 1) * hk]
            b8 = b_ref[pl.ds(r0 + s0, SUBLANES), hcols]
            for u in range(SUBLANES):
                s = s0 + u
                decay = jnp.exp(jnp.where(row[:, 0:1] >= s, bh - b8[u:u + 1, :], -jnp.inf))
                val = jnp.sum(q * k8[u:u + 1, :] * decay, axis=-1, keepdims=True)
                sc = jnp.where(col == s, val, sc)
            return sc

        return lax.fori_loop(0, CHUNK // SUBLANES, columns, jnp.zeros((CHUNK, CHUNK), F32))

    def chunk_step(r0, exact):
        rows = pl.ds(r0, CHUNK)
        for hd in range(GLA_HEADS):
            bh = b_ref[rows, hd * hk:(hd + 1) * hk]
            q = proj_ref[rows, hd * hk:(hd + 1) * hk] * qscale
            k = proj_ref[rows, dk + hd * hk:dk + (hd + 1) * hk]
            v = proj_ref[rows, 2 * dk + hd * hv:2 * dk + (hd + 1) * hv].astype(BF16)
            b_last = bh[CHUNK - 1:CHUNK, :]
            qt = (q * jnp.exp(bh)).astype(BF16)
            if exact:
                scores = exact_scores(r0, rows, hd)
            else:
                kt = (k * jnp.exp(-bh)).astype(BF16)
                scores = jnp.where(causal, _dot_nt(qt, kt), 0.0)
            k_end = (k * jnp.exp(b_last - bh)).astype(BF16)
            st = state_ref[hd]
            o = _dot(scores.astype(BF16), v) + _dot_nt(qt, st.astype(BF16))
            state_ref[hd] = jnp.exp(b_last) * st + _dot_tn(v, k_end)
            oacc_ref[rows, hd * hv:(hd + 1) * hv] = _rms(o) * hn_ref[...]

    unsafe = jnp.max(-b_ref[...]) > GLA_SAFE_DECAY

    @pl.when(jnp.logical_not(unsafe))
    def _():
        for c in range(nchunks):
            chunk_step(c * CHUNK, exact=False)

    @pl.when(unsafe)
    def _():
        def body(c, carry):
            chunk_step(pl.multiple_of(c * CHUNK, CHUNK), exact=True)
            return carry

        lax.fori_loop(0, nchunks, body, 0)

    r = proj_ref[:, 2 * dk + dv:2 * dk + 2 * dv]
    og = (oacc_ref[...] * _silu(r)).astype(BF16)
    o_ref[0] = x + gt_ref[0] * _dot(og, wout_ref[...])


def _gla_call(x, sh, sc, gt, norm_g, w_in, w_gate2, b_gate2, head_norm, w_out, *, blk):
    bsz, s, d = x.shape
    dk = w_gate2.shape[1]
    dv = w_out.shape[0]
    nq = 2 * dk + 2 * dv
    w_qkvr = w_in[:, :nq].astype(BF16)
    w_glr = jnp.pad(w_in[:, nq:], ((0, 0), (0, LANES - GLA_GATE_RANK))).astype(BF16)
    w_g2 = jnp.pad(w_gate2, ((0, LANES - GLA_GATE_RANK), (0, 0))).astype(BF16)
    row3 = pl.BlockSpec((1, 1, d), lambda b, l: (b, 0, 0))
    xspec = pl.BlockSpec((1, blk, d), lambda b, l: (b, l, 0))
    return pl.pallas_call(
        functools.partial(_gla_kernel, dk=dk, dv=dv),
        grid=(bsz, s // blk),
        in_specs=[
            xspec, row3, row3, row3,
            _const_spec((1, d)),
            _const_spec((d, nq)),
            _const_spec((d, LANES)),
            _const_spec((LANES, dk)),
            _const_spec((1, dk)),
            _const_spec((1, dv // GLA_HEADS)),
            _const_spec((dv, d)),
        ],
        out_specs=xspec,
        out_shape=jax.ShapeDtypeStruct((bsz, s, d), F32),
        scratch_shapes=[
            pltpu.VMEM((blk, nq), F32),
            pltpu.VMEM((blk, dk), F32),
            pltpu.VMEM((blk, dv), F32),
            pltpu.VMEM((GLA_HEADS, dv // GLA_HEADS, dk // GLA_HEADS), F32),
            pltpu.VMEM((blk, dk), F32),
        ],
        compiler_params=pltpu.CompilerParams(
            dimension_semantics=("arbitrary", "arbitrary"), vmem_limit_bytes=VMEM_LIMIT),
        name="gla_layer",
    )(x, sh, sc, gt, norm_g.reshape(1, d), w_qkvr, w_glr, w_g2, b_gate2.reshape(1, dk),
      head_norm.reshape(1, -1), w_out.astype(BF16))


def _ssd_kernel(x_ref, sh_ref, sc_ref, gt_ref, nm_ref, wz_ref, wxbc_ref, wdt_ref, cw_ref, cb_ref,
                dtb_ref, alog_ref, dfull_ref, gn_ref, wout_ref, pairsel_ref,
                o_ref, pad_ref, xbc_ref, dt_ref, y_ref, state_ref, *, inner, nheads):
    blk = x_ref.shape[1]
    gs = SSD_GROUPS * SSD_STATE
    gw = inner // SSD_GROUPS
    first = pl.program_id(1) == 0

    @pl.when(first)
    def _():
        state_ref[...] = jnp.zeros_like(state_ref)
        pad_ref[0:SUBLANES, :] = jnp.zeros((SUBLANES, pad_ref.shape[1]), F32)

    x = x_ref[0]
    h = _ada_norm(x, nm_ref[...], sh_ref[0], sc_ref[0]).astype(BF16)

    pad_ref[SUBLANES:SUBLANES + blk, :] = _dot(h, wxbc_ref[...])
    conv = cb_ref[...] + cw_ref[SSD_CONV - 1:SSD_CONV, :] * pad_ref[SUBLANES:SUBLANES + blk, :]
    padded = pad_ref[0:SUBLANES + blk, :]
    for kk in range(SSD_CONV - 1):
        conv = conv + cw_ref[kk:kk + 1, :] * pltpu.roll(padded, SSD_CONV - 1 - kk, 0)[SUBLANES:SUBLANES + blk, :]
    xbc_ref[...] = _silu(conv)
    pad_ref[0:SUBLANES, :] = pad_ref[blk:blk + SUBLANES, :]

    dt_ref[...] = _softplus(_dot(h, wdt_ref[...]) + dtb_ref[...])
    a_neg = -jnp.exp(alog_ref[...])
    lane = lax.broadcasted_iota(jnp.int32, (1, LANES), 1)
    a_neg = jnp.where(lane < nheads, a_neg, 0.0)

    tri = _tri(CHUNK, BF16)
    row2 = lax.broadcasted_iota(jnp.int32, (CHUNK, 2 * CHUNK), 0)
    col2 = lax.broadcasted_iota(jnp.int32, (CHUNK, 2 * CHUNK), 1)
    left = col2 < CHUNK
    causal2 = jnp.where(left, col2, col2 - CHUNK) <= row2
    head_lane = lax.broadcasted_iota(jnp.int32, (CHUNK, LANES), 1)
    even = (head_lane % 2) == 0
    npairs = nheads // 2
    hpg = nheads // SSD_GROUPS

    def chunk_body(c, carry):
        r0 = pl.multiple_of(c * CHUNK, CHUNK)
        rows = pl.ds(r0, CHUNK)
        dt = dt_ref[rows, :]
        acum = _dot_hl(tri, dt * a_neg)
        a_last = acum[CHUNK - 1:CHUNK, :]
        w_end = dt * jnp.exp(a_last - acum)
        def pair_rows(v):
            vv = jnp.concatenate([jnp.where(even, v, 0.0), jnp.where(even, 0.0, v)], axis=0)
            hi, lo = _split2(vv)
            return _dot_nt(pairsel_ref[...], hi) + _dot_nt(pairsel_ref[...], lo)
        acum_t = pair_rows(acum)
        dt_t = pair_rows(dt)

        def pair_cols(v, h0):
            return jnp.where(left, v[:, h0:h0 + 1], v[:, h0 + 1:h0 + 2])

        pw = 2 * SSD_HEADDIM
        for g in range(SSD_GROUPS):
            bm = xbc_ref[rows, inner + g * SSD_STATE:inner + (g + 1) * SSD_STATE].astype(BF16)
            cm = xbc_ref[rows, inner + gs + g * SSD_STATE:inner + gs + (g + 1) * SSD_STATE].astype(BF16)
            cb2 = _dot_nt(cm, jnp.concatenate([bm, bm], axis=0))
            st = state_ref[g]
            c_st = _dot(cm, st.astype(BF16))
            xw_parts, decay_parts = [], []
            for pj in range(hpg // 2):
                j = g * (hpg // 2) + pj
                h0 = 2 * j
                a_col = pair_cols(acum, h0)
                e_col = jnp.exp(a_col)
                xp = xbc_ref[rows, j * pw:(j + 1) * pw]
                xw_parts.append((xp * pair_cols(w_end, h0)).astype(BF16))
                decay_parts.append(e_col[CHUNK - 1:CHUNK, :])
                seg = a_col - acum_t[j:j + 1, :]
                lmat = jnp.exp(jnp.where(causal2, seg, -jnp.inf))
                wgt = (cb2 * lmat * dt_t[j:j + 1, :]).astype(BF16)
                rhs = jnp.concatenate([jnp.where(left, xp, 0.0), jnp.where(left, 0.0, xp)], axis=0)
                y_diag = _dot(wgt, rhs.astype(BF16))
                y_off = e_col * c_st[:, pj * pw:(pj + 1) * pw]
                y_ref[rows, j * pw:(j + 1) * pw] = y_diag + y_off + dfull_ref[:, j * pw:(j + 1) * pw] * xp
            state_ref[g] = (jnp.concatenate(decay_parts, axis=1) * st
                            + _dot_tn(bm, jnp.concatenate(xw_parts, axis=1)))
        return carry

    lax.fori_loop(0, blk // CHUNK, chunk_body, 0, unroll=True)

    z = _dot(h, wz_ref[...])
    y = y_ref[...] * _silu(z)
    yn = jnp.concatenate([_rms(y[:, g * gw:(g + 1) * gw]) for g in range(SSD_GROUPS)], axis=1)
    yn = (yn * gn_ref[...]).astype(BF16)
    o_ref[0] = x + gt_ref[0] * _dot(yn, wout_ref[...])


def _ssd_call(x, sh, sc, gt, norm_g, w_in, conv_w, conv_b, dt_bias, a_log, d_skip, gnorm, w_out, *, blk):
    bsz, s, d = x.shape
    inner = w_out.shape[0]
    nheads = dt_bias.shape[0]
    gs = SSD_GROUPS * SSD_STATE
    cch = inner + 2 * gs
    assert nheads <= LANES and nheads % (2 * SSD_GROUPS) == 0 and 2 * SSD_HEADDIM == LANES
    w_z = w_in[:, :inner].astype(BF16)
    w_xbc = w_in[:, inner:inner + cch].astype(BF16)
    w_dt = jnp.pad(w_in[:, inner + cch:], ((0, 0), (0, LANES - nheads))).astype(BF16)
    dtb = jnp.pad(dt_bias, (0, LANES - nheads)).reshape(1, LANES)
    alog = jnp.pad(a_log, (0, LANES - nheads)).reshape(1, LANES)
    d_full = jnp.repeat(d_skip, SSD_HEADDIM).reshape(1, inner)
    npair_pad = max(SUBLANES, nheads // 2)
    pairsel = ((jnp.arange(LANES)[None, :] // 2) == jnp.arange(npair_pad)[:, None]).astype(BF16)
    row3 = pl.BlockSpec((1, 1, d), lambda b, l: (b, 0, 0))
    xspec = pl.BlockSpec((1, blk, d), lambda b, l: (b, l, 0))
    return pl.pallas_call(
        functools.partial(_ssd_kernel, inner=inner, nheads=nheads),
        grid=(bsz, s // blk),
        in_specs=[
            xspec, row3, row3, row3,
            _const_spec((1, d)),
            _const_spec((d, inner)),
            _const_spec((d, cch)),
            _const_spec((d, LANES)),
            _const_spec((SSD_CONV, cch)),
            _const_spec((1, cch)),
            _const_spec((1, LANES)),
            _const_spec((1, LANES)),
            _const_spec((1, inner)),
            _const_spec((1, inner)),
            _const_spec((inner, d)),
            _const_spec((npair_pad, LANES)),
        ],
        out_specs=xspec,
        out_shape=jax.ShapeDtypeStruct((bsz, s, d), F32),
        scratch_shapes=[
            pltpu.VMEM((blk + 2 * SUBLANES, cch), F32),
            pltpu.VMEM((blk, cch), F32),
            pltpu.VMEM((blk, LANES), F32),
            pltpu.VMEM((blk, inner), F32),
            pltpu.VMEM((SSD_GROUPS, SSD_STATE, inner // SSD_GROUPS), F32),
        ],
        compiler_params=pltpu.CompilerParams(
            dimension_semantics=("arbitrary", "arbitrary"), vmem_limit_bytes=VMEM_LIMIT),
        name="ssd_layer",
    )(x, sh, sc, gt, norm_g.reshape(1, d), w_z, w_xbc, w_dt, conv_w, conv_b.reshape(1, cch),
      dtb, alog, d_full, gnorm.reshape(1, inner), w_out.astype(BF16), pairsel)


def _route(logits_t, rb_ref):
    scores = _sigmoid(logits_t)
    sc = [scores[e:e + 1, :] for e in range(N_EXPERTS)]
    bi = [sc[e] + rb_ref[e:e + 1, :] for e in range(N_EXPERTS)]
    gscore = []
    for g in range(N_GROUPS):
        a, b, c, d = bi[4 * g:4 * g + 4]
        hi1, lo1 = jnp.maximum(a, b), jnp.minimum(a, b)
        hi2, lo2 = jnp.maximum(c, d), jnp.minimum(c, d)
        gscore.append(jnp.maximum(hi1, hi2) + jnp.maximum(jnp.minimum(hi1, hi2), jnp.maximum(lo1, lo2)))
    gates = []
    flags = []
    for g in range(N_GROUPS):
        gsel = None
        for o in range(N_GROUPS):
            if o == g:
                continue
            win = (gscore[g] >= gscore[o]) if o > g else (gscore[g] > gscore[o])
            gsel = win if gsel is None else jnp.logical_and(gsel, win)
        sel = []
        for i in range(EXPERTS_PER_GROUP):
            e = 4 * g + i
            rank = jnp.zeros_like(bi[e])
            for j in range(EXPERTS_PER_GROUP):
                if j == i:
                    continue
                o = 4 * g + j
                beats = (bi[o] >= bi[e]) if j < i else (bi[o] > bi[e])
                rank = rank + jnp.where(beats, 1.0, 0.0)
            chosen = jnp.logical_and(gsel, rank < 2.0)
            flags.append(jnp.where(chosen, 1.0, 0.0))
            sel.append(jnp.where(chosen, sc[e], 0.0))
        denom = sel[0] + sel[1] + sel[2] + sel[3]
        denom = jnp.where(gsel, denom, 1.0)
        gates.extend([s_ / denom for s_ in sel])
    return gates, flags


def _router_kernel(x_ref, sh_ref, sc_ref, nm_ref, rwt_ref, rb_ref, upper_ref,
                   h_ref, slot_ref, col_ref, cnt_ref):
    tm = x_ref.shape[0]
    hb = _ada_norm(x_ref[...], nm_ref[...], sh_ref[0], sc_ref[0]).astype(BF16)
    h_ref[...] = hb
    gates, flags = _route(_dot_nt(rwt_ref[...], hb), rb_ref)
    big = float(N_EXPERTS)
    e_lo = functools.reduce(jnp.minimum, [jnp.where(flags[e] > 0.0, float(e), big) for e in range(N_EXPERTS)])
    e_hi = functools.reduce(jnp.maximum, [jnp.where(flags[e] > 0.0, float(e), -1.0) for e in range(N_EXPERTS)])
    flag_mat = jnp.concatenate(flags, axis=0)
    prefix = _dot(flag_mat.astype(BF16), upper_ref[...])
    counts = jnp.sum(flag_mat, axis=1, keepdims=True)
    slot_of = []
    before = jnp.zeros((1, 1), F32)
    for e in range(N_EXPERTS):
        slot_of.append(prefix[e:e + 1, :] + before)
        before = before + jnp.floor((counts[e:e + 1, :] + (SUBLANES - 1)) * (1.0 / SUBLANES)) * SUBLANES

    def pick(eid, rows):
        return functools.reduce(
            jnp.add, [jnp.where(eid == float(e), rows[e], 0.0) for e in range(N_EXPERTS)])

    zero = jnp.zeros((1, tm), F32)
    s_lo, s_hi = pick(e_lo, slot_of), pick(e_hi, slot_of)
    slot_ref[0] = jnp.concatenate([s_lo, s_hi] + [zero] * (SUBLANES - 2), axis=0).astype(jnp.int32)
    rows = jnp.concatenate([s_lo, s_hi, pick(e_lo, gates), pick(e_hi, gates)] + [zero] * (N_EXPERTS - 4), axis=0)
    col_ref[...] = rows.T
    cnt_ref[0] = jnp.broadcast_to(counts, (N_EXPERTS, LANES))


def _router_call(x2d, sh, sc, norm_g, router_w, router_b, *, tm, tiles_per_batch):
    t, d = x2d.shape
    nt = t // tm
    row3 = pl.BlockSpec((1, 1, d), lambda i: (i // tiles_per_batch, 0, 0))
    rb = jnp.broadcast_to(router_b.reshape(N_EXPERTS, 1), (N_EXPERTS, tm))
    upper = (jnp.arange(tm)[:, None] < jnp.arange(tm)[None, :]).astype(BF16)
    return pl.pallas_call(
        _router_kernel,
        grid=(nt,),
        in_specs=[
            pl.BlockSpec((tm, d), lambda i: (i, 0)), row3, row3,
            _const_spec((1, d)),
            _const_spec((N_EXPERTS, d)),
            _const_spec((N_EXPERTS, tm)),
            _const_spec((tm, tm)),
        ],
        out_specs=[
            pl.BlockSpec((tm, d), lambda i: (i, 0)),
            pl.BlockSpec((1, SUBLANES, tm), lambda i: (i, 0, 0)),
            pl.BlockSpec((tm, N_EXPERTS), lambda i: (i, 0)),
            pl.BlockSpec((1, N_EXPERTS, LANES), lambda i: (i, 0, 0)),
        ],
        out_shape=[
            jax.ShapeDtypeStruct((t, d), BF16),
            jax.ShapeDtypeStruct((nt, SUBLANES, tm), jnp.int32),
            jax.ShapeDtypeStruct((t, N_EXPERTS), F32),
            jax.ShapeDtypeStruct((nt, N_EXPERTS, LANES), F32),
        ],
        compiler_params=pltpu.CompilerParams(dimension_semantics=("arbitrary",), vmem_limit_bytes=VMEM_LIMIT),
        name="moe_router",
    )(x2d, sh, sc, norm_g.reshape(1, d), router_w.T.astype(BF16), rb, upper)


def _pack_pair(a, b):
    ua = lax.bitcast_convert_type(a, jnp.uint32)
    ub = lax.bitcast_convert_type(b, jnp.uint32)
    return (ua >> 16) | (ub & jnp.uint32(0xFFFF0000))


def _unpack_pair(w):
    a = lax.bitcast_convert_type(w << 16, F32)
    b = lax.bitcast_convert_type(w & jnp.uint32(0xFFFF0000), F32)
    return a.astype(BF16), b.astype(BF16)


def _bf16_exact(x):
    return x.astype(BF16).astype(F32)


def _slot_rows(tm):
    return 2 * tm + N_EXPERTS * SUBLANES


def _run_copies(n_rows, max_rows, make_copy, op):
    for b in reversed(range(SUBLANES.bit_length() - 1, max_rows.bit_length())):
        size = 1 << b
        offset = pl.multiple_of((n_rows >> (b + 1)) << (b + 1), SUBLANES)

        @pl.when((n_rows & size) != 0)
        def _(size=size, offset=offset):
            op(make_copy(offset, size))


def _tile_runs_start(tile, tm, len_ref, lbase_ref, toff_ref, local_ref, global_ref, sem, to_global):
    for e in range(N_EXPERTS):
        idx = tile * N_EXPERTS + e
        lo = lbase_ref[idx]
        go = toff_ref[idx]

        def make_copy(offset, size, lo=lo, go=go):
            loc = local_ref.at[pl.ds(pl.multiple_of(lo + offset, SUBLANES), size)]
            glb = global_ref.at[pl.ds(pl.multiple_of(go + offset, SUBLANES), size)]
            return pltpu.make_async_copy(loc, glb, sem) if to_global else pltpu.make_async_copy(glb, loc, sem)

        _run_copies(len_ref[idx], tm, make_copy, _start)


def _tile_runs_wait(tile, tm, rows_ref, local_ref, global_ref, sem, to_global):
    def make_copy(offset, size):
        del offset
        loc = local_ref.at[pl.ds(0, size)]
        glb = global_ref.at[pl.ds(0, size)]
        return pltpu.make_async_copy(loc, glb, sem) if to_global else pltpu.make_async_copy(glb, loc, sem)

    _run_copies(rows_ref[tile], _slot_rows(tm), make_copy, _wait)


def _start(copy):
    copy.start()


def _wait(copy):
    copy.wait()


def _dispatch_kernel(cnt_ref, lbase_ref, toff_ref, rows_ref, padstart_ref, padlen_ref, nact_ref,
                     slot_ref, h_ref, xs_ref, buf_ref, zero_ref, sem, zsem, *, rt):
    i = pl.program_id(0)
    last = pl.num_programs(0) - 1
    cur = i % 2
    tm = h_ref.shape[0]
    half = h_ref.shape[1] // 2

    nslot = _slot_rows(tm)

    def wait_runs(tile, which):
        _tile_runs_wait(tile, tm, rows_ref, buf_ref.at[which], xs_ref, sem.at[which], True)

    @pl.when(i > 0)
    def _():
        wait_runs(i - 1, 1 - cur)

    sl = slot_ref[0]
    srow = lax.broadcasted_iota(jnp.int32, (nslot, tm), 0)
    perm = jnp.where(srow == sl[0:1, :], 1.0, jnp.where(srow == sl[1:2, :], 1.0, 0.0)).astype(BF16)
    rows = _dot(perm, h_ref[...])
    buf_ref[cur] = _pack_pair(rows[:, :half], rows[:, half:])
    _tile_runs_start(i, tm, cnt_ref, lbase_ref, toff_ref, buf_ref.at[cur], xs_ref, sem.at[cur], True)

    @pl.when(i == last)
    def _():
        wait_runs(i, cur)
        zero_ref[...] = jnp.zeros_like(zero_ref)
        for op in (_start, _wait):
            for e in range(N_EXPERTS):
                def make_copy(offset, size, e=e):
                    dst = xs_ref.at[pl.ds(pl.multiple_of(padstart_ref[e] + offset, SUBLANES), size)]
                    return pltpu.make_async_copy(zero_ref.at[pl.ds(0, size)], dst, zsem)
                _run_copies(padlen_ref[e], rt - 1, make_copy, op)

        def zero_tile(j, carry):
            copy = pltpu.make_async_copy(zero_ref, xs_ref.at[pl.ds(pl.multiple_of(j * rt, rt), rt)], zsem)
            copy.start()
            copy.wait()
            return carry

        lax.fori_loop(nact_ref[0], xs_ref.shape[0] // rt, zero_tile, 0)


def _dispatch_call(seg, slot_rows, h2d, n_rows, *, rt):
    t, d = h2d.shape
    nt, _, tm = slot_rows.shape
    grid_spec = pltpu.PrefetchScalarGridSpec(
        num_scalar_prefetch=7,
        grid=(nt,),
        in_specs=[
            pl.BlockSpec((1, SUBLANES, tm), lambda i, *_: (i, 0, 0)),
            pl.BlockSpec((tm, d), lambda i, *_: (i, 0)),
        ],
        out_specs=pl.BlockSpec(memory_space=pl.ANY),
        scratch_shapes=[
            pltpu.VMEM((2, _slot_rows(tm), d // 2), jnp.uint32),
            pltpu.VMEM((rt, d // 2), jnp.uint32),
            pltpu.SemaphoreType.DMA((2,)),
            pltpu.SemaphoreType.DMA,
        ],
    )
    return pl.pallas_call(
        functools.partial(_dispatch_kernel, rt=rt),
        grid_spec=grid_spec,
        out_shape=jax.ShapeDtypeStruct((n_rows, d // 2), jnp.uint32),
        compiler_params=pltpu.CompilerParams(dimension_semantics=("arbitrary",), vmem_limit_bytes=VMEM_LIMIT),
        name="moe_dispatch",
    )(seg["cnt"], seg["lbase"], seg["toff"], seg["rows"], seg["padstart"], seg["padlen"], seg["nact"],
      slot_rows, h2d)


def _expert_kernel(te_ref, nact_ref, xs_ref, wg_ref, wu_ref, wd_ref, o_ref, wgb_ref, wub_ref, wdb_ref):
    j = pl.program_id(0)
    active = j < nact_ref[0]
    half = xs_ref.shape[1]

    @pl.when(jnp.logical_and(active, jnp.logical_or(j == 0, te_ref[j] != te_ref[jnp.maximum(j - 1, 0)])))
    def _():
        wgb_ref[...] = wg_ref[0, 0].astype(BF16)
        wub_ref[...] = wu_ref[0, 0].astype(BF16)
        wdb_ref[...] = wd_ref[0, 0].astype(BF16)

    @pl.when(active)
    def _():
        xa, xb = _unpack_pair(xs_ref[...])
        gate = _dot(xa, wgb_ref[0:half, :]) + _dot(xb, wgb_ref[half:, :])
        up = _dot(xa, wub_ref[0:half, :]) + _dot(xb, wub_ref[half:, :])
        he = (_silu(gate) * up).astype(BF16)
        y = _dot(he, wdb_ref[...])
        o_ref[...] = _pack_pair(_bf16_exact(y[:, :half]), _bf16_exact(y[:, half:]))

    @pl.when(jnp.logical_not(active))
    def _():
        o_ref[...] = jnp.zeros_like(o_ref)


def _expert_call(tile_expert, nact, xs, w_gate, w_up, w_down, *, layer, rt):
    n_rows, half = xs.shape
    _, _, d, de = w_gate.shape

    def row_map(j, te, na):
        return (jnp.minimum(j, na[0] - 1), 0)

    grid_spec = pltpu.PrefetchScalarGridSpec(
        num_scalar_prefetch=2,
        grid=(n_rows // rt,),
        in_specs=[
            pl.BlockSpec((rt, half), row_map),
            pl.BlockSpec((1, 1, d, de), lambda j, te, na: (layer, te[j], 0, 0)),
            pl.BlockSpec((1, 1, d, de), lambda j, te, na: (layer, te[j], 0, 0)),
            pl.BlockSpec((1, 1, de, d), lambda j, te, na: (layer, te[j], 0, 0)),
        ],
        out_specs=pl.BlockSpec((rt, half), lambda j, te, na: (j, 0)),
        scratch_shapes=[pltpu.VMEM((d, de), BF16), pltpu.VMEM((d, de), BF16), pltpu.VMEM((de, d), BF16)],
    )
    return pl.pallas_call(
        _expert_kernel,
        grid_spec=grid_spec,
        out_shape=jax.ShapeDtypeStruct((n_rows, half), jnp.uint32),
        compiler_params=pltpu.CompilerParams(dimension_semantics=("arbitrary",), vmem_limit_bytes=VMEM_LIMIT),
        name="moe_experts",
    )(tile_expert, nact, xs, w_gate, w_up, w_down)


def _combine_kernel(cnt_ref, lbase_ref, toff_ref, rows_ref, x_ref, gt_ref, col_ref, fin_ref, ys_ref, o_ref,
                    buf_ref, sem, *, final):
    i = pl.program_id(0)
    n = pl.num_programs(0)
    cur = i % 2
    tm = x_ref.shape[0]

    nslot = _slot_rows(tm)

    def start_runs(tile, which):
        _tile_runs_start(tile, tm, cnt_ref, lbase_ref, toff_ref, buf_ref.at[which], ys_ref, sem.at[which], False)

    @pl.when(i == 0)
    def _():
        buf_ref[...] = jnp.zeros_like(buf_ref)
        start_runs(i, cur)

    @pl.when(i + 1 < n)
    def _():
        start_runs(i + 1, 1 - cur)

    _tile_runs_wait(i, tm, rows_ref, buf_ref.at[cur], ys_ref, sem.at[cur], False)
    ya, yb = _unpack_pair(buf_ref[cur])
    col = col_ref[...]
    lane = lax.broadcasted_iota(jnp.int32, (tm, nslot), 1).astype(F32)
    unsort = jnp.where(lane == col[:, 0:1], col[:, 2:3],
                       jnp.where(lane == col[:, 1:2], col[:, 3:4], 0.0)).astype(BF16)
    moe = jnp.concatenate([_dot(unsort, ya), _dot(unsort, yb)], axis=1)
    y = x_ref[...] + gt_ref[0] * moe
    if final:
        y = _rms(y) * fin_ref[...]
    o_ref[...] = y


def _combine_call(seg, x2d, gt, cols, norm_final, ys, *, tm, tiles_per_batch, final):
    t, d = x2d.shape
    nt = t // tm
    grid_spec = pltpu.PrefetchScalarGridSpec(
        num_scalar_prefetch=4,
        grid=(nt,),
        in_specs=[
            pl.BlockSpec((tm, d), lambda i, *_: (i, 0)),
            pl.BlockSpec((1, 1, d), lambda i, *_: (i // tiles_per_batch, 0, 0)),
            pl.BlockSpec((tm, N_EXPERTS), lambda i, *_: (i, 0)),
            pl.BlockSpec((1, d), lambda i, *_: (0, 0)),
            pl.BlockSpec(memory_space=pl.ANY),
        ],
        out_specs=pl.BlockSpec((tm, d), lambda i, *_: (i, 0)),
        scratch_shapes=[pltpu.VMEM((2, _slot_rows(tm), d // 2), jnp.uint32), pltpu.SemaphoreType.DMA((2,))],
    )
    return pl.pallas_call(
        functools.partial(_combine_kernel, final=final),
        grid_spec=grid_spec,
        out_shape=jax.ShapeDtypeStruct((t, d), F32),
        compiler_params=pltpu.CompilerParams(dimension_semantics=("arbitrary",), vmem_limit_bytes=VMEM_LIMIT),
        name="moe_combine",
    )(seg["cnt"], seg["lbase"], seg["toff"], seg["rows"], x2d, gt, cols, norm_final.reshape(1, d), ys)


def _moe_layer(x, sh, sc, gt, norm_g, router_w, router_b, w_gate, w_up, w_down, norm_final, *,
               layer, tm, rt, final):
    bsz, s, d = x.shape
    t = bsz * s
    x2d = x.reshape(t, d)
    tiles_per_batch = s // tm
    h2d, slot_rows, cols, cnt = _router_call(x2d, sh, sc, norm_g, router_w, router_b, tm=tm,
                                             tiles_per_batch=tiles_per_batch)
    cnt = cnt[:, :, 0].astype(jnp.int32)
    run = ((cnt + SUBLANES - 1) // SUBLANES) * SUBLANES
    total = jnp.sum(run, axis=0)
    padded = ((total + rt - 1) // rt) * rt
    ends = jnp.cumsum(padded)
    base = ends - padded
    nt = t // tm
    n_rows = -(-(2 * t + nt * N_EXPERTS * SUBLANES) // rt) * rt + N_EXPERTS * rt
    starts = jnp.arange(n_rows // rt, dtype=jnp.int32) * rt
    tile_expert = jnp.minimum(jnp.sum(starts[:, None] >= ends[None, :], axis=1), N_EXPERTS - 1).astype(jnp.int32)
    nact = (ends[-1:] // rt).astype(jnp.int32)
    seg = {
        "cnt": run.reshape(-1),
        "lbase": (jnp.cumsum(run, axis=1) - run).reshape(-1),
        "toff": (base[None, :] + jnp.cumsum(run, axis=0) - run).reshape(-1),
        "rows": jnp.sum(run, axis=1),
        "padstart": base + total,
        "padlen": padded - total,
        "nact": nact,
    }
    seg = {k: v.astype(jnp.int32) for k, v in seg.items()}
    xs = _dispatch_call(seg, slot_rows, h2d, n_rows, rt=rt)
    ys = _expert_call(tile_expert, nact, xs, w_gate, w_up, w_down, layer=layer, rt=rt)
    out = _combine_call(seg, x2d, gt, cols, norm_final, ys, tm=tm, tiles_per_batch=tiles_per_batch, final=final)
    return out.reshape(bsz, s, d)


def kernel(x, c, ada_w, ada_b, norm_mix, norm_ffn, norm_final, gla_w_in, gla_w_gate2, gla_b_gate2, gla_norm,
           gla_w_out, ssd_w_in, ssd_conv_w, ssd_conv_b, ssd_dt_bias, ssd_a_log, ssd_d, ssd_norm, ssd_w_out,
           router_w, router_b, moe_w_gate, moe_w_up, moe_w_down):
    depth = ada_w.shape[0]
    bsz, s, d = x.shape
    mod = _ada_call(c, ada_w, ada_b).reshape(depth, bsz, 6, 1, d)
    n_mixers = 2
    for i in range(depth):
        sh1, sc1, g1, sh2, sc2, g2 = (mod[i, :, t] for t in range(6))
        j = i // n_mixers
        if i % n_mixers == 0:
            x = _gla_call(x, sh1, sc1, g1, norm_mix[i], gla_w_in[j], gla_w_gate2[j], gla_b_gate2[j],
                          gla_norm[j], gla_w_out[j], blk=min(512, s))
        else:
            x = _ssd_call(x, sh1, sc1, g1, norm_mix[i], ssd_w_in[j], ssd_conv_w[j], ssd_conv_b[j],
                          ssd_dt_bias[j], ssd_a_log[j], ssd_d[j], ssd_norm[j], ssd_w_out[j], blk=min(256, s))
        x = _moe_layer(x, sh2, sc2, g2, norm_ffn[i], router_w, router_b, moe_w_gate, moe_w_up, moe_w_down,
                       norm_final, layer=i, tm=min(256, s), rt=512, final=(i == depth - 1))
    return x
```

```python
import functools

import jax
import jax.numpy as jnp
from jax import lax
from jax.experimental import pallas as pl
from jax.experimental.pallas import tpu as pltpu

F32 = jnp.float32
BF16 = jnp.bfloat16

EPS = 1e-6
CHUNK = 64

GLA_HEADS = 4
GLA_GATE_RANK = 16
GLA_GATE_TAU = 16.0
GLA_SAFE_DECAY = 40.0

SSD_HEADDIM = 64
SSD_GROUPS = 4
SSD_STATE = 128
SSD_CONV = 4

N_EXPERTS = 16
N_GROUPS = 4
EXPERTS_PER_GROUP = N_EXPERTS // N_GROUPS

LANES = 128
SUBLANES = 8
VMEM_LIMIT = 56 * 1024 * 1024


def _dot(a, b):
    return jnp.dot(a, b, preferred_element_type=F32)


def _dot_nt(a, b):
    return lax.dot_general(a, b, (((1,), (1,)), ((), ())), preferred_element_type=F32)


def _dot_tn(a, b):
    return lax.dot_general(a, b, (((0,), (0,)), ((), ())), preferred_element_type=F32)


def _split2(x):
    hi = x.astype(BF16)
    lo = (x - hi.astype(F32)).astype(BF16)
    return hi, lo


def _dot_hl(a_bf16, x):
    hi, lo = _split2(x)
    return _dot(a_bf16, hi) + _dot(a_bf16, lo)


def _sigmoid(x):
    return 1.0 / (1.0 + jnp.exp(-x))


def _silu(x):
    return x * (0.5 * jnp.tanh(0.5 * x) + 0.5)


def _softplus(x):
    return jnp.maximum(x, 0.0) + jnp.log(1.0 + jnp.exp(-jnp.abs(x)))


def _log_sigmoid(x):
    return jnp.minimum(x, 0.0) - jnp.log(1.0 + jnp.exp(-jnp.abs(x)))


def _rms(x):
    return x * lax.rsqrt(jnp.mean(x * x, axis=-1, keepdims=True) + EPS)


def _ada_norm(x, g, shift, scale):
    return _rms(x) * g * (1.0 + scale) + shift


def _const_spec(shape):
    nd = len(shape)
    return pl.BlockSpec(shape, lambda *_: (0,) * nd, pipeline_mode=pl.Buffered(1))


def _tri(n, dtype):
    r = lax.broadcasted_iota(jnp.int32, (n, n), 0)
    c = lax.broadcasted_iota(jnp.int32, (n, n), 1)
    return (c <= r).astype(dtype)


def _ada_kernel(c_ref, w_ref, b_ref, o_ref):
    cond = _silu(c_ref[...]).astype(BF16)
    o_ref[0] = _dot(cond, w_ref[0].astype(BF16)) + b_ref[0]


def _ada_call(c, ada_w, ada_b):
    depth, d, n = ada_w.shape
    bsz = c.shape[0]
    tn = 1536
    return pl.pallas_call(
        _ada_kernel,
        grid=(depth, n // tn),
        in_specs=[
            pl.BlockSpec((bsz, d), lambda i, j: (0, 0)),
            pl.BlockSpec((1, d, tn), lambda i, j: (i, 0, j)),
            pl.BlockSpec((1, 1, tn), lambda i, j: (i, 0, j)),
        ],
        out_specs=pl.BlockSpec((1, bsz, tn), lambda i, j: (i, 0, j)),
        out_shape=jax.ShapeDtypeStruct((depth, bsz, n), F32),
        compiler_params=pltpu.CompilerParams(
            dimension_semantics=("arbitrary", "arbitrary"), vmem_limit_bytes=VMEM_LIMIT),
        name="ada_mod",
    )(c, ada_w, ada_b.reshape(depth, 1, n))


def _gla_kernel(x_ref, sh_ref, sc_ref, gt_ref, nm_ref, wqkvr_ref, wglr_ref, wg2_ref, bg2_ref,
                hn_ref, wout_ref, o_ref, proj_ref, loga_ref, oacc_ref, state_ref, b_ref, *, dk, dv):
    hk = dk // GLA_HEADS
    hv = dv // GLA_HEADS
    blk = x_ref.shape[1]

    @pl.when(pl.program_id(1) == 0)
    def _():
        state_ref[...] = jnp.zeros_like(state_ref)

    x = x_ref[0]
    h = _ada_norm(x, nm_ref[...], sh_ref[0], sc_ref[0]).astype(BF16)
    proj_ref[...] = _dot(h, wqkvr_ref[...])
    glr = _dot(h, wglr_ref[...]).astype(BF16)
    z = _dot(glr, wg2_ref[...]) + bg2_ref[...]
    loga_ref[...] = _log_sigmoid(z) * (1.0 / GLA_GATE_TAU)

    tri = _tri(CHUNK, BF16)
    row = lax.broadcasted_iota(jnp.int32, (CHUNK, CHUNK), 0)
    col = lax.broadcasted_iota(jnp.int32, (CHUNK, CHUNK), 1)
    causal = col <= row
    qscale = hk ** -0.5

    nchunks = blk // CHUNK
    for c in range(nchunks):
        b_ref[c * CHUNK:(c + 1) * CHUNK, :] = _dot_hl(tri, loga_ref[c * CHUNK:(c + 1) * CHUNK, :])

    def exact_scores(r0, rows, hd):
        hcols = slice(hd * hk, (hd + 1) * hk)
        q = proj_ref[rows, hcols] * qscale
        bh = b_ref[rows, hcols]

        def columns(g, sc):
            s0 = pl.multiple_of(g * SUBLANES, SUBLANES)
            k8 = proj_ref[pl.ds(r0 + s0, SUBLANES), dk + hd * hk:dk + (hd + 1) * hk]
            b8 = b_ref[pl.ds(r0 + s0, SUBLANES), hcols]
            for u in range(SUBLANES):
                s = s0 + u
                decay = jnp.exp(jnp.where(row[:, 0:1] >= s, bh - b8[u:u + 1, :], -jnp.inf))
                val = jnp.sum(q * k8[u:u + 1, :] * decay, axis=-1, keepdims=True)
                sc = jnp.where(col == s, val, sc)
            return sc

        return lax.fori_loop(0, CHUNK // SUBLANES, columns, jnp.zeros((CHUNK, CHUNK), F32))

    def chunk_step(r0, exact):
        rows = pl.ds(r0, CHUNK)
        for hd in range(GLA_HEADS):
            bh = b_ref[rows, hd * hk:(hd + 1) * hk]
            q = proj_ref[rows, hd * hk:(hd + 1) * hk] * qscale
            k = proj_ref[rows, dk + hd * hk:dk + (hd + 1) * hk]
            v = proj_ref[rows, 2 * dk + hd * hv:2 * dk + (hd + 1) * hv].astype(BF16)
            b_last = bh[CHUNK - 1:CHUNK, :]
            qt = (q * jnp.exp(bh)).astype(BF16)
            if exact:
                scores = exact_scores(r0, rows, hd)
            else:
                kt = (k * jnp.exp(-bh)).astype(BF16)
                scores = jnp.where(causal, _dot_nt(qt, kt), 0.0)
            k_end = (k * jnp.exp(b_last - bh)).astype(BF16)
            st = state_ref[hd]
            o = _dot(scores.astype(BF16), v) + _dot_nt(qt, st.astype(BF16))
            state_ref[hd] = jnp.exp(b_last) * st + _dot_tn(v, k_end)
            oacc_ref[rows, hd * hv:(hd + 1) * hv] = _rms(o) * hn_ref[...]

    unsafe = jnp.max(-b_ref[...]) > GLA_SAFE_DECAY

    @pl.when(jnp.logical_not(unsafe))
    def _():
        for c in range(nchunks):
            chunk_step(c * CHUNK, exact=False)

    @pl.when(unsafe)
    def _():
        def body(c, carry):
            chunk_step(pl.multiple_of(c * CHUNK, CHUNK), exact=True)
            return carry

        lax.fori_loop(0, nchunks, body, 0)

    r = proj_ref[:, 2 * dk + dv:2 * dk + 2 * dv]
    og = (oacc_ref[...] * _silu(r)).astype(BF16)
    o_ref[0] = x + gt_ref[0] * _dot(og, wout_ref[...])


def _gla_call(x, sh, sc, gt, norm_g, w_in, w_gate2, b_gate2, head_norm, w_out, *, blk):
    bsz, s, d = x.shape
    dk = w_gate2.shape[1]
    dv = w_out.shape[0]
    nq = 2 * dk + 2 * dv
    w_qkvr = w_in[:, :nq].astype(BF16)
    w_glr = jnp.pad(w_in[:, nq:], ((0, 0), (0, LANES - GLA_GATE_RANK))).astype(BF16)
    w_g2 = jnp.pad(w_gate2, ((0, LANES - GLA_GATE_RANK), (0, 0))).astype(BF16)
    row3 = pl.BlockSpec((1, 1, d), lambda b, l: (b, 0, 0))
    xspec = pl.BlockSpec((1, blk, d), lambda b, l: (b, l, 0))
    return pl.pallas_call(
        functools.partial(_gla_kernel, dk=dk, dv=dv),
        grid=(bsz, s // blk),
        in_specs=[
            xspec, row3, row3, row3,
            _const_spec((1, d)),
            _const_spec((d, nq)),
            _const_spec((d, LANES)),
            _const_spec((LANES, dk)),
            _const_spec((1, dk)),
            _const_spec((1, dv // GLA_HEADS)),
            _const_spec((dv, d)),
        ],
        out_specs=xspec,
        out_shape=jax.ShapeDtypeStruct((bsz, s, d), F32),
        scratch_shapes=[
            pltpu.VMEM((blk, nq), F32),
            pltpu.VMEM((blk, dk), F32),
            pltpu.VMEM((blk, dv), F32),
            pltpu.VMEM((GLA_HEADS, dv // GLA_HEADS, dk // GLA_HEADS), F32),
            pltpu.VMEM((blk, dk), F32),
        ],
        compiler_params=pltpu.CompilerParams(
            dimension_semantics=("arbitrary", "arbitrary"), vmem_limit_bytes=VMEM_LIMIT),
        name="gla_layer",
    )(x, sh, sc, gt, norm_g.reshape(1, d), w_qkvr, w_glr, w_g2, b_gate2.reshape(1, dk),
      head_norm.reshape(1, -1), w_out.astype(BF16))


def _ssd_kernel(x_ref, sh_ref, sc_ref, gt_ref, nm_ref, wz_ref, wxbc_ref, wdt_ref, cw_ref, cb_ref,
                dtb_ref, alog_ref, dfull_ref, gn_ref, wout_ref, pairsel_ref,
                o_ref, pad_ref, xbc_ref, dt_ref, y_ref, state_ref, *, inner, nheads):
    blk = x_ref.shape[1]
    gs = SSD_GROUPS * SSD_STATE
    gw = inner // SSD_GROUPS
    first = pl.program_id(1) == 0

    @pl.when(first)
    def _():
        state_ref[...] = jnp.zeros_like(state_ref)
        pad_ref[0:SUBLANES, :] = jnp.zeros((SUBLANES, pad_ref.shape[1]), F32)

    x = x_ref[0]
    h = _ada_norm(x, nm_ref[...], sh_ref[0], sc_ref[0]).astype(BF16)

    pad_ref[SUBLANES:SUBLANES + blk, :] = _dot(h, wxbc_ref[...])
    conv = cb_ref[...] + cw_ref[SSD_CONV - 1:SSD_CONV, :] * pad_ref[SUBLANES:SUBLANES + blk, :]
    padded = pad_ref[0:SUBLANES + blk, :]
    for kk in range(SSD_CONV - 1):
        conv = conv + cw_ref[kk:kk + 1, :] * pltpu.roll(padded, SSD_CONV - 1 - kk, 0)[SUBLANES:SUBLANES + blk, :]
    xbc_ref[...] = _silu(conv)
    pad_ref[0:SUBLANES, :] = pad_ref[blk:blk + SUBLANES, :]

    dt_ref[...] = _softplus(_dot(h, wdt_ref[...]) + dtb_ref[...])
    a_neg = -jnp.exp(alog_ref[...])
    lane = lax.broadcasted_iota(jnp.int32, (1, LANES), 1)
    a_neg = jnp.where(lane < nheads, a_neg, 0.0)

    tri = _tri(CHUNK, BF16)
    row2 = lax.broadcasted_iota(jnp.int32, (CHUNK, 2 * CHUNK), 0)
    col2 = lax.broadcasted_iota(jnp.int32, (CHUNK, 2 * CHUNK), 1)
    left = col2 < CHUNK
    causal2 = jnp.where(left, col2, col2 - CHUNK) <= row2
    head_lane = lax.broadcasted_iota(jnp.int32, (CHUNK, LANES), 1)
    even = (head_lane % 2) == 0
    npairs = nheads // 2
    hpg = nheads // SSD_GROUPS

    def chunk_body(c, carry):
        r0 = pl.multiple_of(c * CHUNK, CHUNK)
        rows = pl.ds(r0, CHUNK)
        dt = dt_ref[rows, :]
        acum = _dot_hl(tri, dt * a_neg)
        a_last = acum[CHUNK - 1:CHUNK, :]
        w_end = dt * jnp.exp(a_last - acum)
        def pair_rows(v):
            vv = jnp.concatenate([jnp.where(even, v, 0.0), jnp.where(even, 0.0, v)], axis=0)
            hi, lo = _split2(vv)
            return _dot_nt(pairsel_ref[...], hi) + _dot_nt(pairsel_ref[...], lo)
        acum_t = pair_rows(acum)
        dt_t = pair_rows(dt)

        def pair_cols(v, h0):
            return jnp.where(left, v[:, h0:h0 + 1], v[:, h0 + 1:h0 + 2])

        pw = 2 * SSD_HEADDIM
        for g in range(SSD_GROUPS):
            bm = xbc_ref[rows, inner + g * SSD_STATE:inner + (g + 1) * SSD_STATE].astype(BF16)
            cm = xbc_ref[rows, inner + gs + g * SSD_STATE:inner + gs + (g + 1) * SSD_STATE].astype(BF16)
            cb2 = _dot_nt(cm, jnp.concatenate([bm, bm], axis=0))
            st = state_ref[g]
            c_st = _dot(cm, st.astype(BF16))
            xw_parts, decay_parts = [], []
            for pj in range(hpg // 2):
                j = g * (hpg // 2) + pj
                h0 = 2 * j
                a_col = pair_cols(acum, h0)
                e_col = jnp.exp(a_col)
                xp = xbc_ref[rows, j * pw:(j + 1) * pw]
                xw_parts.append((xp * pair_cols(w_end, h0)).astype(BF16))
                decay_parts.append(e_col[CHUNK - 1:CHUNK, :])
                seg = a_col - acum_t[j:j + 1, :]
                lmat = jnp.exp(jnp.where(causal2, seg, -jnp.inf))
                wgt = (cb2 * lmat * dt_t[j:j + 1, :]).astype(BF16)
                rhs = jnp.concatenate([jnp.where(left, xp, 0.0), jnp.where(left, 0.0, xp)], axis=0)
                y_diag = _dot(wgt, rhs.astype(BF16))
                y_off = e_col * c_st[:, pj * pw:(pj + 1) * pw]
                y_ref[rows, j * pw:(j + 1) * pw] = y_diag + y_off + dfull_ref[:, j * pw:(j + 1) * pw] * xp
            state_ref[g] = (jnp.concatenate(decay_parts, axis=1) * st
                            + _dot_tn(bm, jnp.concatenate(xw_parts, axis=1)))
        return carry

    lax.fori_loop(0, blk // CHUNK, chunk_body, 0, unroll=True)

    z = _dot(h, wz_ref[...])
    y = y_ref[...] * _silu(z)
    yn = jnp.concatenate([_rms(y[:, g * gw:(g + 1) * gw]) for g in range(SSD_GROUPS)], axis=1)
    yn = (yn * gn_ref[...]).astype(BF16)
    o_ref[0] = x + gt_ref[0] * _dot(yn, wout_ref[...])


def _ssd_call(x, sh, sc, gt, norm_g, w_in, conv_w, conv_b, dt_bias, a_log, d_skip, gnorm, w_out, *, blk):
    bsz, s, d = x.shape
    inner = w_out.shape[0]
    nheads = dt_bias.shape[0]
    gs = SSD_GROUPS * SSD_STATE
    cch = inner + 2 * gs
    assert nheads <= LANES and nheads % (2 * SSD_GROUPS) == 0 and 2 * SSD_HEADDIM == LANES
    w_z = w_in[:, :inner].astype(BF16)
    w_xbc = w_in[:, inner:inner + cch].astype(BF16)
    w_dt = jnp.pad(w_in[:, inner + cch:], ((0, 0), (0, LANES - nheads))).astype(BF16)
    dtb = jnp.pad(dt_bias, (0, LANES - nheads)).reshape(1, LANES)
    alog = jnp.pad(a_log, (0, LANES - nheads)).reshape(1, LANES)
    d_full = jnp.repeat(d_skip, SSD_HEADDIM).reshape(1, inner)
    npair_pad = max(SUBLANES, nheads // 2)
    pairsel = ((jnp.arange(LANES)[None, :] // 2) == jnp.arange(npair_pad)[:, None]).astype(BF16)
    row3 = pl.BlockSpec((1, 1, d), lambda b, l: (b, 0, 0))
    xspec = pl.BlockSpec((1, blk, d), lambda b, l: (b, l, 0))
    return pl.pallas_call(
        functools.partial(_ssd_kernel, inner=inner, nheads=nheads),
        grid=(bsz, s // blk),
        in_specs=[
            xspec, row3, row3, row3,
            _const_spec((1, d)),
            _const_spec((d, inner)),
            _const_spec((d, cch)),
            _const_spec((d, LANES)),
            _const_spec((SSD_CONV, cch)),
            _const_spec((1, cch)),
            _const_spec((1, LANES)),
            _const_spec((1, LANES)),
            _const_spec((1, inner)),
            _const_spec((1, inner)),
            _const_spec((inner, d)),
            _const_spec((npair_pad, LANES)),
        ],
        out_specs=xspec,
        out_shape=jax.ShapeDtypeStruct((bsz, s, d), F32),
        scratch_shapes=[
            pltpu.VMEM((blk + 2 * SUBLANES, cch), F32),
            pltpu.VMEM((blk, cch), F32),
            pltpu.VMEM((blk, LANES), F32),
            pltpu.VMEM((blk, inner), F32),
            pltpu.VMEM((SSD_GROUPS, SSD_STATE, inner // SSD_GROUPS), F32),
        ],
        compiler_params=pltpu.CompilerParams(
            dimension_semantics=("arbitrary", "arbitrary"), vmem_limit_bytes=VMEM_LIMIT),
        name="ssd_layer",
    )(x, sh, sc, gt, norm_g.reshape(1, d), w_z, w_xbc, w_dt, conv_w, conv_b.reshape(1, cch),
      dtb, alog, d_full, gnorm.reshape(1, inner), w_out.astype(BF16), pairsel)


def _route(logits_t, rb_ref):
    scores = _sigmoid(logits_t)
    sc = [scores[e:e + 1, :] for e in range(N_EXPERTS)]
    bi = [sc[e] + rb_ref[e:e + 1, :] for e in range(N_EXPERTS)]
    gscore = []
    for g in range(N_GROUPS):
        a, b, c, d = bi[4 * g:4 * g + 4]
        hi1, lo1 = jnp.maximum(a, b), jnp.minimum(a, b)
        hi2, lo2 = jnp.maximum(c, d), jnp.minimum(c, d)
        gscore.append(jnp.maximum(hi1, hi2) + jnp.maximum(jnp.minimum(hi1, hi2), jnp.maximum(lo1, lo2)))
    gates = []
    flags = []
    for g in range(N_GROUPS):
        gsel = None
        for o in range(N_GROUPS):
            if o == g:
                continue
            win = (gscore[g] >= gscore[o]) if o > g else (gscore[g] > gscore[o])
            gsel = win if gsel is None else jnp.logical_and(gsel, win)
        sel = []
        for i in range(EXPERTS_PER_GROUP):
            e = 4 * g + i
            rank = jnp.zeros_like(bi[e])
            for j in range(EXPERTS_PER_GROUP):
                if j == i:
                    continue
                o = 4 * g + j
                beats = (bi[o] >= bi[e]) if j < i else (bi[o] > bi[e])
                rank = rank + jnp.where(beats, 1.0, 0.0)
            chosen = jnp.logical_and(gsel, rank < 2.0)
            flags.append(jnp.where(chosen, 1.0, 0.0))
            sel.append(jnp.where(chosen, sc[e], 0.0))
        denom = sel[0] + sel[1] + sel[2] + sel[3]
        denom = jnp.where(gsel, denom, 1.0)
        gates.extend([s_ / denom for s_ in sel])
    return gates, flags


def _router_kernel(x_ref, sh_ref, sc_ref, nm_ref, rwt_ref, rb_ref, upper_ref,
                   h_ref, slot_ref, col_ref, cnt_ref):
    tm = x_ref.shape[0]
    hb = _ada_norm(x_ref[...], nm_ref[...], sh_ref[0], sc_ref[0]).astype(BF16)
    h_ref[...] = hb
    gates, flags = _route(_dot_nt(rwt_ref[...], hb), rb_ref)
    big = float(N_EXPERTS)
    e_lo = functools.reduce(jnp.minimum, [jnp.where(flags[e] > 0.0, float(e), big) for e in range(N_EXPERTS)])
    e_hi = functools.reduce(jnp.maximum, [jnp.where(flags[e] > 0.0, float(e), -1.0) for e in range(N_EXPERTS)])
    flag_mat = jnp.concatenate(flags, axis=0)
    prefix = _dot(flag_mat.astype(BF16), upper_ref[...])
    counts = jnp.sum(flag_mat, axis=1, keepdims=True)
    slot_of = []
    before = jnp.zeros((1, 1), F32)
    for e in range(N_EXPERTS):
        slot_of.append(prefix[e:e + 1, :] + before)
        before = before + jnp.floor((counts[e:e + 1, :] + (SUBLANES - 1)) * (1.0 / SUBLANES)) * SUBLANES

    def pick(eid, rows):
        return functools.reduce(
            jnp.add, [jnp.where(eid == float(e), rows[e], 0.0) for e in range(N_EXPERTS)])

    zero = jnp.zeros((1, tm), F32)
    s_lo, s_hi = pick(e_lo, slot_of), pick(e_hi, slot_of)
    slot_ref[0] = jnp.concatenate([s_lo, s_hi] + [zero] * (SUBLANES - 2), axis=0).astype(jnp.int32)
    rows = jnp.concatenate([s_lo, s_hi, pick(e_lo, gates), pick(e_hi, gates)] + [zero] * (N_EXPERTS - 4), axis=0)
    col_ref[...] = rows.T
    cnt_ref[0] = jnp.broadcast_to(counts, (N_EXPERTS, LANES))


def _router_call(x2d, sh, sc, norm_g, router_w, router_b, *, tm, tiles_per_batch):
    t, d = x2d.shape
    nt = t // tm
    row3 = pl.BlockSpec((1, 1, d), lambda i: (i // tiles_per_batch, 0, 0))
    rb = jnp.broadcast_to(router_b.reshape(N_EXPERTS, 1), (N_EXPERTS, tm))
    upper = (jnp.arange(tm)[:, None] < jnp.arange(tm)[None, :]).astype(BF16)
    return pl.pallas_call(
        _router_kernel,
        grid=(nt,),
        in_specs=[
            pl.BlockSpec((tm, d), lambda i: (i, 0)), row3, row3,
            _const_spec((1, d)),
            _const_spec((N_EXPERTS, d)),
            _const_spec((N_EXPERTS, tm)),
            _const_spec((tm, tm)),
        ],
        out_specs=[
            pl.BlockSpec((tm, d), lambda i: (i, 0)),
            pl.BlockSpec((1, SUBLANES, tm), lambda i: (i, 0, 0)),
            pl.BlockSpec((tm, N_EXPERTS), lambda i: (i, 0)),
            pl.BlockSpec((1, N_EXPERTS, LANES), lambda i: (i, 0, 0)),
        ],
        out_shape=[
            jax.ShapeDtypeStruct((t, d), BF16),
            jax.ShapeDtypeStruct((nt, SUBLANES, tm), jnp.int32),
            jax.ShapeDtypeStruct((t, N_EXPERTS), F32),
            jax.ShapeDtypeStruct((nt, N_EXPERTS, LANES), F32),
        ],
        compiler_params=pltpu.CompilerParams(dimension_semantics=("arbitrary",), vmem_limit_bytes=VMEM_LIMIT),
        name="moe_router",
    )(x2d, sh, sc, norm_g.reshape(1, d), router_w.T.astype(BF16), rb, upper)


def _pack_pair(a, b):
    ua = lax.bitcast_convert_type(a, jnp.uint32)
    ub = lax.bitcast_convert_type(b, jnp.uint32)
    return (ua >> 16) | (ub & jnp.uint32(0xFFFF0000))


def _unpack_pair(w):
    a = lax.bitcast_convert_type(w << 16, F32)
    b = lax.bitcast_convert_type(w & jnp.uint32(0xFFFF0000), F32)
    return a.astype(BF16), b.astype(BF16)


def _bf16_exact(x):
    return x.astype(BF16).astype(F32)


def _slot_rows(tm):
    return 2 * tm + N_EXPERTS * SUBLANES


def _run_copies(n_rows, max_rows, make_copy, op):
    for b in reversed(range(SUBLANES.bit_length() - 1, max_rows.bit_length())):
        size = 1 << b
        offset = pl.multiple_of((n_rows >> (b + 1)) << (b + 1), SUBLANES)

        @pl.when((n_rows & size) != 0)
        def _(size=size, offset=offset):
            op(make_copy(offset, size))


def _tile_runs_start(tile, tm, len_ref, lbase_ref, toff_ref, local_ref, global_ref, sem, to_global):
    for e in range(N_EXPERTS):
        idx = tile * N_EXPERTS + e
        lo = lbase_ref[idx]
        go = toff_ref[idx]

        def make_copy(offset, size, lo=lo, go=go):
            loc = local_ref.at[pl.ds(pl.multiple_of(lo + offset, SUBLANES), size)]
            glb = global_ref.at[pl.ds(pl.multiple_of(go + offset, SUBLANES), size)]
            return pltpu.make_async_copy(loc, glb, sem) if to_global else pltpu.make_async_copy(glb, loc, sem)

        _run_copies(len_ref[idx], tm, make_copy, _start)


def _tile_runs_wait(tile, tm, rows_ref, local_ref, global_ref, sem, to_global):
    def make_copy(offset, size):
        del offset
        loc = local_ref.at[pl.ds(0, size)]
        glb = global_ref.at[pl.ds(0, size)]
        return pltpu.make_async_copy(loc, glb, sem) if to_global else pltpu.make_async_copy(glb, loc, sem)

    _run_copies(rows_ref[tile], _slot_rows(tm), make_copy, _wait)


def _start(copy):
    copy.start()


def _wait(copy):
    copy.wait()


def _dispatch_kernel(cnt_ref, lbase_ref, toff_ref, rows_ref, padstart_ref, padlen_ref, nact_ref,
                     slot_ref, h_ref, xs_ref, buf_ref, zero_ref, sem, zsem, *, rt):
    i = pl.program_id(0)
    last = pl.num_programs(0) - 1
    cur = i % 2
    tm = h_ref.shape[0]
    half = h_ref.shape[1] // 2

    nslot = _slot_rows(tm)

    def wait_runs(tile, which):
        _tile_runs_wait(tile, tm, rows_ref, buf_ref.at[which], xs_ref, sem.at[which], True)

    @pl.when(i > 0)
    def _():
        wait_runs(i - 1, 1 - cur)

    sl = slot_ref[0]
    srow = lax.broadcasted_iota(jnp.int32, (nslot, tm), 0)
    perm = jnp.where(srow == sl[0:1, :], 1.0, jnp.where(srow == sl[1:2, :], 1.0, 0.0)).astype(BF16)
    rows = _dot(perm, h_ref[...])
    buf_ref[cur] = _pack_pair(rows[:, :half], rows[:, half:])
    _tile_runs_start(i, tm, cnt_ref, lbase_ref, toff_ref, buf_ref.at[cur], xs_ref, sem.at[cur], True)

    @pl.when(i == last)
    def _():
        wait_runs(i, cur)
        zero_ref[...] = jnp.zeros_like(zero_ref)
        for op in (_start, _wait):
            for e in range(N_EXPERTS):
                def make_copy(offset, size, e=e):
                    dst = xs_ref.at[pl.ds(pl.multiple_of(padstart_ref[e] + offset, SUBLANES), size)]
                    return pltpu.make_async_copy(zero_ref.at[pl.ds(0, size)], dst, zsem)
                _run_copies(padlen_ref[e], rt - 1, make_copy, op)

        def zero_tile(j, carry):
            copy = pltpu.make_async_copy(zero_ref, xs_ref.at[pl.ds(pl.multiple_of(j * rt, rt), rt)], zsem)
            copy.start()
            copy.wait()
            return carry

        lax.fori_loop(nact_ref[0], xs_ref.shape[0] // rt, zero_tile, 0)


def _dispatch_call(seg, slot_rows, h2d, n_rows, *, rt):
    t, d = h2d.shape
    nt, _, tm = slot_rows.shape
    grid_spec = pltpu.PrefetchScalarGridSpec(
        num_scalar_prefetch=7,
        grid=(nt,),
        in_specs=[
            pl.BlockSpec((1, SUBLANES, tm), lambda i, *_: (i, 0, 0)),
            pl.BlockSpec((tm, d), lambda i, *_: (i, 0)),
        ],
        out_specs=pl.BlockSpec(memory_space=pl.ANY),
        scratch_shapes=[
            pltpu.VMEM((2, _slot_rows(tm), d // 2), jnp.uint32),
            pltpu.VMEM((rt, d // 2), jnp.uint32),
            pltpu.SemaphoreType.DMA((2,)),
            pltpu.SemaphoreType.DMA,
        ],
    )
    return pl.pallas_call(
        functools.partial(_dispatch_kernel, rt=rt),
        grid_spec=grid_spec,
        out_shape=jax.ShapeDtypeStruct((n_rows, d // 2), jnp.uint32),
        compiler_params=pltpu.CompilerParams(dimension_semantics=("arbitrary",), vmem_limit_bytes=VMEM_LIMIT),
        name="moe_dispatch",
    )(seg["cnt"], seg["lbase"], seg["toff"], seg["rows"], seg["padstart"], seg["padlen"], seg["nact"],
      slot_rows, h2d)


def _expert_kernel(te_ref, nact_ref, xs_ref, wg_ref, wu_ref, wd_ref, o_ref, wgb_ref, wub_ref, wdb_ref):
    j = pl.program_id(0)
    active = j < nact_ref[0]
    half = xs_ref.shape[1]

    @pl.when(jnp.logical_and(active, jnp.logical_or(j == 0, te_ref[j] != te_ref[jnp.maximum(j - 1, 0)])))
    def _():
        wgb_ref[...] = wg_ref[0, 0].astype(BF16)
        wub_ref[...] = wu_ref[0, 0].astype(BF16)
        wdb_ref[...] = wd_ref[0, 0].astype(BF16)

    @pl.when(active)
    def _():
        xa, xb = _unpack_pair(xs_ref[...])
        gate = _dot(xa, wgb_ref[0:half, :]) + _dot(xb, wgb_ref[half:, :])
        up = _dot(xa, wub_ref[0:half, :]) + _dot(xb, wub_ref[half:, :])
        he = (_silu(gate) * up).astype(BF16)
        y = _dot(he, wdb_ref[...])
        o_ref[...] = _pack_pair(_bf16_exact(y[:, :half]), _bf16_exact(y[:, half:]))

    @pl.when(jnp.logical_not(active))
    def _():
        o_ref[...] = jnp.zeros_like(o_ref)


def _expert_call(tile_expert, nact, xs, w_gate, w_up, w_down, *, layer, rt):
    n_rows, half = xs.shape
    _, _, d, de = w_gate.shape

    def row_map(j, te, na):
        return (jnp.minimum(j, na[0] - 1), 0)

    grid_spec = pltpu.PrefetchScalarGridSpec(
        num_scalar_prefetch=2,
        grid=(n_rows // rt,),
        in_specs=[
            pl.BlockSpec((rt, half), row_map),
            pl.BlockSpec((1, 1, d, de), lambda j, te, na: (layer, te[j], 0, 0)),
            pl.BlockSpec((1, 1, d, de), lambda j, te, na: (layer, te[j], 0, 0)),
            pl.BlockSpec((1, 1, de, d), lambda j, te, na: (layer, te[j], 0, 0)),
        ],
        out_specs=pl.BlockSpec((rt, half), lambda j, te, na: (j, 0)),
        scratch_shapes=[pltpu.VMEM((d, de), BF16), pltpu.VMEM((d, de), BF16), pltpu.VMEM((de, d), BF16)],
    )
    return pl.pallas_call(
        _expert_kernel,
        grid_spec=grid_spec,
        out_shape=jax.ShapeDtypeStruct((n_rows, half), jnp.uint32),
        compiler_params=pltpu.CompilerParams(dimension_semantics=("arbitrary",), vmem_limit_bytes=VMEM_LIMIT),
        name="moe_experts",
    )(tile_expert, nact, xs, w_gate, w_up, w_down)


def _combine_kernel(cnt_ref, lbase_ref, toff_ref, rows_ref, x_ref, gt_ref, col_ref, fin_ref, ys_ref, o_ref,
                    buf_ref, sem, *, final):
    i = pl.program_id(0)
    n = pl.num_programs(0)
    cur = i % 2
    tm = x_ref.shape[0]

    nslot = _slot_rows(tm)

    def start_runs(tile, which):
        _tile_runs_start(tile, tm, cnt_ref, lbase_ref, toff_ref, buf_ref.at[which], ys_ref, sem.at[which], False)

    @pl.when(i == 0)
    def _():
        buf_ref[...] = jnp.zeros_like(buf_ref)
        start_runs(i, cur)

    @pl.when(i + 1 < n)
    def _():
        start_runs(i + 1, 1 - cur)

    _tile_runs_wait(i, tm, rows_ref, buf_ref.at[cur], ys_ref, sem.at[cur], False)
    ya, yb = _unpack_pair(buf_ref[cur])
    col = col_ref[...]
    lane = lax.broadcasted_iota(jnp.int32, (tm, nslot), 1).astype(F32)
    unsort = jnp.where(lane == col[:, 0:1], col[:, 2:3],
                       jnp.where(lane == col[:, 1:2], col[:, 3:4], 0.0)).astype(BF16)
    moe = jnp.concatenate([_dot(unsort, ya), _dot(unsort, yb)], axis=1)
    y = x_ref[...] + gt_ref[0] * moe
    if final:
        y = _rms(y) * fin_ref[...]
    o_ref[...] = y


def _combine_call(seg, x2d, gt, cols, norm_final, ys, *, tm, tiles_per_batch, final):
    t, d = x2d.shape
    nt = t // tm
    grid_spec = pltpu.PrefetchScalarGridSpec(
        num_scalar_prefetch=4,
        grid=(nt,),
        in_specs=[
            pl.BlockSpec((tm, d), lambda i, *_: (i, 0)),
            pl.BlockSpec((1, 1, d), lambda i, *_: (i // tiles_per_batch, 0, 0)),
            pl.BlockSpec((tm, N_EXPERTS), lambda i, *_: (i, 0)),
            pl.BlockSpec((1, d), lambda i, *_: (0, 0)),
            pl.BlockSpec(memory_space=pl.ANY),
        ],
        out_specs=pl.BlockSpec((tm, d), lambda i, *_: (i, 0)),
        scratch_shapes=[pltpu.VMEM((2, _slot_rows(tm), d // 2), jnp.uint32), pltpu.SemaphoreType.DMA((2,))],
    )
    return pl.pallas_call(
        functools.partial(_combine_kernel, final=final),
        grid_spec=grid_spec,
        out_shape=jax.ShapeDtypeStruct((t, d), F32),
        compiler_params=pltpu.CompilerParams(dimension_semantics=("arbitrary",), vmem_limit_bytes=VMEM_LIMIT),
        name="moe_combine",
    )(seg["cnt"], seg["lbase"], seg["toff"], seg["rows"], x2d, gt, cols, norm_final.reshape(1, d), ys)


def _moe_layer(x, sh, sc, gt, norm_g, router_w, router_b, w_gate, w_up, w_down, norm_final, *,
               layer, tm, rt, final):
    bsz, s, d = x.shape
    t = bsz * s
    x2d = x.reshape(t, d)
    tiles_per_batch = s // tm
    h2d, slot_rows, cols, cnt = _router_call(x2d, sh, sc, norm_g, router_w, router_b, tm=tm,
                                             tiles_per_batch=tiles_per_batch)
    cnt = cnt[:, :, 0].astype(jnp.int32)
    run = ((cnt + SUBLANES - 1) // SUBLANES) * SUBLANES
    total = jnp.sum(run, axis=0)
    padded = ((total + rt - 1) // rt) * rt
    ends = jnp.cumsum(padded)
    base = ends - padded
    nt = t // tm
    n_rows = -(-(2 * t + nt * N_EXPERTS * SUBLANES) // rt) * rt + N_EXPERTS * rt
    starts = jnp.arange(n_rows // rt, dtype=jnp.int32) * rt
    tile_expert = jnp.minimum(jnp.sum(starts[:, None] >= ends[None, :], axis=1), N_EXPERTS - 1).astype(jnp.int32)
    nact = (ends[-1:] // rt).astype(jnp.int32)
    seg = {
        "cnt": run.reshape(-1),
        "lbase": (jnp.cumsum(run, axis=1) - run).reshape(-1),
        "toff": (base[None, :] + jnp.cumsum(run, axis=0) - run).reshape(-1),
        "rows": jnp.sum(run, axis=1),
        "padstart": base + total,
        "padlen": padded - total,
        "nact": nact,
    }
    seg = {k: v.astype(jnp.int32) for k, v in seg.items()}
    xs = _dispatch_call(seg, slot_rows, h2d, n_rows, rt=rt)
    ys = _expert_call(tile_expert, nact, xs, w_gate, w_up, w_down, layer=layer, rt=rt)
    out = _combine_call(seg, x2d, gt, cols, norm_final, ys, tm=tm, tiles_per_batch=tiles_per_batch, final=final)
    return out.reshape(bsz, s, d)


def kernel(x, c, ada_w, ada_b, norm_mix, norm_ffn, norm_final, gla_w_in, gla_w_gate2, gla_b_gate2, gla_norm,
           gla_w_out, ssd_w_in, ssd_conv_w, ssd_conv_b, ssd_dt_bias, ssd_a_log, ssd_d, ssd_norm, ssd_w_out,
           router_w, router_b, moe_w_gate, moe_w_up, moe_w_down):
    depth = ada_w.shape[0]
    bsz, s, d = x.shape
    mod = _ada_call(c, ada_w, ada_b).reshape(depth, bsz, 6, 1, d)
    n_mixers = 2
    for i in range(depth):
        sh1, sc1, g1, sh2, sc2, g2 = (mod[i, :, t] for t in range(6))
        j = i // n_mixers
        if i % n_mixers == 0:
            x = _gla_call(x, sh1, sc1, g1, norm_mix[i], gla_w_in[j], gla_w_gate2[j], gla_b_gate2[j],
                          gla_norm[j], gla_w_out[j], blk=min(1024, s))
        else:
            x = _ssd_call(x, sh1, sc1, g1, norm_mix[i], ssd_w_in[j], ssd_conv_w[j], ssd_conv_b[j],
                          ssd_dt_bias[j], ssd_a_log[j], ssd_d[j], ssd_norm[j], ssd_w_out[j], blk=min(256, s))
        x = _moe_layer(x, sh2, sc2, g2, norm_ffn[i], router_w, router_b, moe_w_gate, moe_w_up, moe_w_down,
                       norm_final, layer=i, tm=min(256, s), rt=512, final=(i == depth - 1))
    return x
```

```python
import functools

import jax
import jax.numpy as jnp
from jax import lax
from jax.experimental import pallas as pl
from jax.experimental.pallas import tpu as pltpu

F32 = jnp.float32
BF16 = jnp.bfloat16

EPS = 1e-6
CHUNK = 64

GLA_HEADS = 4
GLA_GATE_RANK = 16
GLA_GATE_TAU = 16.0
GLA_SAFE_DECAY = 40.0

SSD_HEADDIM = 64
SSD_GROUPS = 4
SSD_STATE = 128
SSD_CONV = 4

N_EXPERTS = 16
N_GROUPS = 4
EXPERTS_PER_GROUP = N_EXPERTS // N_GROUPS

LANES = 128
SUBLANES = 8
VMEM_LIMIT = 56 * 1024 * 1024


def _dot(a, b):
    return jnp.dot(a, b, preferred_element_type=F32)


def _dot_nt(a, b):
    return lax.dot_general(a, b, (((1,), (1,)), ((), ())), preferred_element_type=F32)


def _dot_tn(a, b):
    return lax.dot_general(a, b, (((0,), (0,)), ((), ())), preferred_element_type=F32)


def _split2(x):
    hi = x.astype(BF16)
    lo = (x - hi.astype(F32)).astype(BF16)
    return hi, lo


def _dot_hl(a_bf16, x):
    hi, lo = _split2(x)
    return _dot(a_bf16, hi) + _dot(a_bf16, lo)


def _sigmoid(x):
    return 1.0 / (1.0 + jnp.exp(-x))


def _silu(x):
    return x * (0.5 * jnp.tanh(0.5 * x) + 0.5)


def _softplus(x):
    return jnp.maximum(x, 0.0) + jnp.log(1.0 + jnp.exp(-jnp.abs(x)))


def _log_sigmoid(x):
    return jnp.minimum(x, 0.0) - jnp.log(1.0 + jnp.exp(-jnp.abs(x)))


def _rms(x):
    return x * lax.rsqrt(jnp.mean(x * x, axis=-1, keepdims=True) + EPS)


def _ada_norm(x, g, shift, scale):
    return _rms(x) * g * (1.0 + scale) + shift


def _const_spec(shape):
    nd = len(shape)
    return pl.BlockSpec(shape, lambda *_: (0,) * nd, pipeline_mode=pl.Buffered(1))


def _tri(n, dtype):
    r = lax.broadcasted_iota(jnp.int32, (n, n), 0)
    c = lax.broadcasted_iota(jnp.int32, (n, n), 1)
    return (c <= r).astype(dtype)


def _ada_kernel(c_ref, w_ref, b_ref, o_ref):
    cond = _silu(c_ref[...]).astype(BF16)
    o_ref[0] = _dot(cond, w_ref[0].astype(BF16)) + b_ref[0]


def _ada_call(c, ada_w, ada_b):
    depth, d, n = ada_w.shape
    bsz = c.shape[0]
    tn = 1536
    return pl.pallas_call(
        _ada_kernel,
        grid=(depth, n // tn),
        in_specs=[
            pl.BlockSpec((bsz, d), lambda i, j: (0, 0)),
            pl.BlockSpec((1, d, tn), lambda i, j: (i, 0, j)),
            pl.BlockSpec((1, 1, tn), lambda i, j: (i, 0, j)),
        ],
        out_specs=pl.BlockSpec((1, bsz, tn), lambda i, j: (i, 0, j)),
        out_shape=jax.ShapeDtypeStruct((depth, bsz, n), F32),
        compiler_params=pltpu.CompilerParams(
            dimension_semantics=("arbitrary", "arbitrary"), vmem_limit_bytes=VMEM_LIMIT),
        name="ada_mod",
    )(c, ada_w, ada_b.reshape(depth, 1, n))


def _gla_kernel(x_ref, sh_ref, sc_ref, gt_ref, nm_ref, wqkvr_ref, wglr_ref, wg2_ref, bg2_ref,
                hn_ref, wout_ref, o_ref, proj_ref, loga_ref, oacc_ref, state_ref, b_ref, *, dk, dv):
    hk = dk // GLA_HEADS
    hv = dv // GLA_HEADS
    blk = x_ref.shape[1]

    @pl.when(pl.program_id(1) == 0)
    def _():
        state_ref[...] = jnp.zeros_like(state_ref)

    x = x_ref[0]
    h = _ada_norm(x, nm_ref[...], sh_ref[0], sc_ref[0]).astype(BF16)
    proj_ref[...] = _dot(h, wqkvr_ref[...])
    glr = _dot(h, wglr_ref[...]).astype(BF16)
    z = _dot(glr, wg2_ref[...]) + bg2_ref[...]
    loga_ref[...] = _log_sigmoid(z) * (1.0 / GLA_GATE_TAU)

    tri = _tri(CHUNK, BF16)
    row = lax.broadcasted_iota(jnp.int32, (CHUNK, CHUNK), 0)
    col = lax.broadcasted_iota(jnp.int32, (CHUNK, CHUNK), 1)
    causal = col <= row
    qscale = hk ** -0.5

    nchunks = blk // CHUNK
    for c in range(nchunks):
        b_ref[c * CHUNK:(c + 1) * CHUNK, :] = _dot_hl(tri, loga_ref[c * CHUNK:(c + 1) * CHUNK, :])

    def exact_scores(r0, rows, hd):
        hcols = slice(hd * hk, (hd + 1) * hk)
        q = proj_ref[rows, hcols] * qscale
        bh = b_ref[rows, hcols]

        def columns(g, sc):
            s0 = pl.multiple_of(g * SUBLANES, SUBLANES)
            k8 = proj_ref[pl.ds(r0 + s0, SUBLANES), dk + hd * hk:dk + (hd + 1) * hk]
            b8 = b_ref[pl.ds(r0 + s0, SUBLANES), hcols]
            for u in range(SUBLANES):
                s = s0 + u
                decay = jnp.exp(jnp.where(row[:, 0:1] >= s, bh - b8[u:u + 1, :], -jnp.inf))
                val = jnp.sum(q * k8[u:u + 1, :] * decay, axis=-1, keepdims=True)
                sc = jnp.where(col == s, val, sc)
            return sc

        return lax.fori_loop(0, CHUNK // SUBLANES, columns, jnp.zeros((CHUNK, CHUNK), F32))

    def chunk_step(r0, exact):
        rows = pl.ds(r0, CHUNK)
        for hd in range(GLA_HEADS):
            bh = b_ref[rows, hd * hk:(hd + 1) * hk]
            q = proj_ref[rows, hd * hk:(hd + 1) * hk] * qscale
            k = proj_ref[rows, dk + hd * hk:dk + (hd + 1) * hk]
            v = proj_ref[rows, 2 * dk + hd * hv:2 * dk + (hd + 1) * hv].astype(BF16)
            b_last = bh[CHUNK - 1:CHUNK, :]
            qt = (q * jnp.exp(bh)).astype(BF16)
            if exact:
                scores = exact_scores(r0, rows, hd)
            else:
                kt = (k * jnp.exp(-bh)).astype(BF16)
                scores = jnp.where(causal, _dot_nt(qt, kt), 0.0)
            k_end = (k * jnp.exp(b_last - bh)).astype(BF16)
            st = state_ref[hd]
            o = _dot(scores.astype(BF16), v) + _dot_nt(qt, st.astype(BF16))
            state_ref[hd] = jnp.exp(b_last) * st + _dot_tn(v, k_end)
            oacc_ref[rows, hd * hv:(hd + 1) * hv] = _rms(o) * hn_ref[...]

    unsafe = jnp.max(-b_ref[...]) > GLA_SAFE_DECAY

    @pl.when(jnp.logical_not(unsafe))
    def _():
        for c in range(nchunks):
            chunk_step(c * CHUNK, exact=False)

    @pl.when(unsafe)
    def _():
        def body(c, carry):
            chunk_step(pl.multiple_of(c * CHUNK, CHUNK), exact=True)
            return carry

        lax.fori_loop(0, nchunks, body, 0)

    r = proj_ref[:, 2 * dk + dv:2 * dk + 2 * dv]
    og = (oacc_ref[...] * _silu(r)).astype(BF16)
    o_ref[0] = x + gt_ref[0] * _dot(og, wout_ref[...])


def _gla_call(x, sh, sc, gt, norm_g, w_in, w_gate2, b_gate2, head_norm, w_out, *, blk):
    bsz, s, d = x.shape
    dk = w_gate2.shape[1]
    dv = w_out.shape[0]
    nq = 2 * dk + 2 * dv
    w_qkvr = w_in[:, :nq].astype(BF16)
    w_glr = jnp.pad(w_in[:, nq:], ((0, 0), (0, LANES - GLA_GATE_RANK))).astype(BF16)
    w_g2 = jnp.pad(w_gate2, ((0, LANES - GLA_GATE_RANK), (0, 0))).astype(BF16)
    row3 = pl.BlockSpec((1, 1, d), lambda b, l: (b, 0, 0))
    xspec = pl.BlockSpec((1, blk, d), lambda b, l: (b, l, 0))
    return pl.pallas_call(
        functools.partial(_gla_kernel, dk=dk, dv=dv),
        grid=(bsz, s // blk),
        in_specs=[
            xspec, row3, row3, row3,
            _const_spec((1, d)),
            _const_spec((d, nq)),
            _const_spec((d, LANES)),
            _const_spec((LANES, dk)),
            _const_spec((1, dk)),
            _const_spec((1, dv // GLA_HEADS)),
            _const_spec((dv, d)),
        ],
        out_specs=xspec,
        out_shape=jax.ShapeDtypeStruct((bsz, s, d), F32),
        scratch_shapes=[
            pltpu.VMEM((blk, nq), F32),
            pltpu.VMEM((blk, dk), F32),
            pltpu.VMEM((blk, dv), F32),
            pltpu.VMEM((GLA_HEADS, dv // GLA_HEADS, dk // GLA_HEADS), F32),
            pltpu.VMEM((blk, dk), F32),
        ],
        compiler_params=pltpu.CompilerParams(
            dimension_semantics=("arbitrary", "arbitrary"), vmem_limit_bytes=VMEM_LIMIT),
        name="gla_layer",
    )(x, sh, sc, gt, norm_g.reshape(1, d), w_qkvr, w_glr, w_g2, b_gate2.reshape(1, dk),
      head_norm.reshape(1, -1), w_out.astype(BF16))


def _ssd_kernel(x_ref, sh_ref, sc_ref, gt_ref, nm_ref, wz_ref, wxbc_ref, wdt_ref, cw_ref, cb_ref,
                dtb_ref, alog_ref, dfull_ref, gn_ref, wout_ref, pairsel_ref,
                o_ref, pad_ref, xbc_ref, dt_ref, y_ref, state_ref, *, inner, nheads):
    blk = x_ref.shape[1]
    gs = SSD_GROUPS * SSD_STATE
    gw = inner // SSD_GROUPS
    first = pl.program_id(1) == 0

    @pl.when(first)
    def _():
        state_ref[...] = jnp.zeros_like(state_ref)
        pad_ref[0:SUBLANES, :] = jnp.zeros((SUBLANES, pad_ref.shape[1]), F32)

    x = x_ref[0]
    h = _ada_norm(x, nm_ref[...], sh_ref[0], sc_ref[0]).astype(BF16)

    pad_ref[SUBLANES:SUBLANES + blk, :] = _dot(h, wxbc_ref[...])
    conv = cb_ref[...] + cw_ref[SSD_CONV - 1:SSD_CONV, :] * pad_ref[SUBLANES:SUBLANES + blk, :]
    padded = pad_ref[0:SUBLANES + blk, :]
    for kk in range(SSD_CONV - 1):
        conv = conv + cw_ref[kk:kk + 1, :] * pltpu.roll(padded, SSD_CONV - 1 - kk, 0)[SUBLANES:SUBLANES + blk, :]
    xbc_ref[...] = _silu(conv)
    pad_ref[0:SUBLANES, :] = pad_ref[blk:blk + SUBLANES, :]

    dt_ref[...] = _softplus(_dot(h, wdt_ref[...]) + dtb_ref[...])
    a_neg = -jnp.exp(alog_ref[...])
    lane = lax.broadcasted_iota(jnp.int32, (1, LANES), 1)
    a_neg = jnp.where(lane < nheads, a_neg, 0.0)

    tri = _tri(CHUNK, BF16)
    row2 = lax.broadcasted_iota(jnp.int32, (CHUNK, 2 * CHUNK), 0)
    col2 = lax.broadcasted_iota(jnp.int32, (CHUNK, 2 * CHUNK), 1)
    left = col2 < CHUNK
    causal2 = jnp.where(left, col2, col2 - CHUNK) <= row2
    head_lane = lax.broadcasted_iota(jnp.int32, (CHUNK, LANES), 1)
    even = (head_lane % 2) == 0
    npairs = nheads // 2
    hpg = nheads // SSD_GROUPS

    def chunk_body(c, carry):
        r0 = pl.multiple_of(c * CHUNK, CHUNK)
        rows = pl.ds(r0, CHUNK)
        dt = dt_ref[rows, :]
        acum = _dot_hl(tri, dt * a_neg)
        a_last = acum[CHUNK - 1:CHUNK, :]
        w_end = dt * jnp.exp(a_last - acum)
        def pair_rows(v):
            vv = jnp.concatenate([jnp.where(even, v, 0.0), jnp.where(even, 0.0, v)], axis=0)
            hi, lo = _split2(vv)
            return _dot_nt(pairsel_ref[...], hi) + _dot_nt(pairsel_ref[...], lo)
        acum_t = pair_rows(acum)
        dt_t = pair_rows(dt)

        def pair_cols(v, h0):
            return jnp.where(left, v[:, h0:h0 + 1], v[:, h0 + 1:h0 + 2])

        pw = 2 * SSD_HEADDIM
        for g in range(SSD_GROUPS):
            bm = xbc_ref[rows, inner + g * SSD_STATE:inner + (g + 1) * SSD_STATE].astype(BF16)
            cm = xbc_ref[rows, inner + gs + g * SSD_STATE:inner + gs + (g + 1) * SSD_STATE].astype(BF16)
            cb2 = _dot_nt(cm, jnp.concatenate([bm, bm], axis=0))
            st = state_ref[g]
            c_st = _dot(cm, st.astype(BF16))
            xw_parts, decay_parts = [], []
            for pj in range(hpg // 2):
                j = g * (hpg // 2) + pj
                h0 = 2 * j
                a_col = pair_cols(acum, h0)
                e_col = jnp.exp(a_col)
                xp = xbc_ref[rows, j * pw:(j + 1) * pw]
                xw_parts.append((xp * pair_cols(w_end, h0)).astype(BF16))
                decay_parts.append(e_col[CHUNK - 1:CHUNK, :])
                seg = a_col - acum_t[j:j + 1, :]
                lmat = jnp.exp(jnp.where(causal2, seg, -jnp.inf))
                wgt = (cb2 * lmat * dt_t[j:j + 1, :]).astype(BF16)
                rhs = jnp.concatenate([jnp.where(left, xp, 0.0), jnp.where(left, 0.0, xp)], axis=0)
                y_diag = _dot(wgt, rhs.astype(BF16))
                y_off = e_col * c_st[:, pj * pw:(pj + 1) * pw]
                y_ref[rows, j * pw:(j + 1) * pw] = y_diag + y_off + dfull_ref[:, j * pw:(j + 1) * pw] * xp
            state_ref[g] = (jnp.concatenate(decay_parts, axis=1) * st
                            + _dot_tn(bm, jnp.concatenate(xw_parts, axis=1)))
        return carry

    lax.fori_loop(0, blk // CHUNK, chunk_body, 0, unroll=True)

    z = _dot(h, wz_ref[...])
    y = y_ref[...] * _silu(z)
    yn = jnp.concatenate([_rms(y[:, g * gw:(g + 1) * gw]) for g in range(SSD_GROUPS)], axis=1)
    yn = (yn * gn_ref[...]).astype(BF16)
    o_ref[0] = x + gt_ref[0] * _dot(yn, wout_ref[...])


def _ssd_call(x, sh, sc, gt, norm_g, w_in, conv_w, conv_b, dt_bias, a_log, d_skip, gnorm, w_out, *, blk):
    bsz, s, d = x.shape
    inner = w_out.shape[0]
    nheads = dt_bias.shape[0]
    gs = SSD_GROUPS * SSD_STATE
    cch = inner + 2 * gs
    assert nheads <= LANES and nheads % (2 * SSD_GROUPS) == 0 and 2 * SSD_HEADDIM == LANES
    w_z = w_in[:, :inner].astype(BF16)
    w_xbc = w_in[:, inner:inner + cch].astype(BF16)
    w_dt = jnp.pad(w_in[:, inner + cch:], ((0, 0), (0, LANES - nheads))).astype(BF16)
    dtb = jnp.pad(dt_bias, (0, LANES - nheads)).reshape(1, LANES)
    alog = jnp.pad(a_log, (0, LANES - nheads)).reshape(1, LANES)
    d_full = jnp.repeat(d_skip, SSD_HEADDIM).reshape(1, inner)
    npair_pad = max(SUBLANES, nheads // 2)
    pairsel = ((jnp.arange(LANES)[None, :] // 2) == jnp.arange(npair_pad)[:, None]).astype(BF16)
    row3 = pl.BlockSpec((1, 1, d), lambda b, l: (b, 0, 0))
    xspec = pl.BlockSpec((1, blk, d), lambda b, l: (b, l, 0))
    return pl.pallas_call(
        functools.partial(_ssd_kernel, inner=inner, nheads=nheads),
        grid=(bsz, s // blk),
        in_specs=[
            xspec, row3, row3, row3,
            _const_spec((1, d)),
            _const_spec((d, inner)),
            _const_spec((d, cch)),
            _const_spec((d, LANES)),
            _const_spec((SSD_CONV, cch)),
            _const_spec((1, cch)),
            _const_spec((1, LANES)),
            _const_spec((1, LANES)),
            _const_spec((1, inner)),
            _const_spec((1, inner)),
            _const_spec((inner, d)),
            _const_spec((npair_pad, LANES)),
        ],
        out_specs=xspec,
        out_shape=jax.ShapeDtypeStruct((bsz, s, d), F32),
        scratch_shapes=[
            pltpu.VMEM((blk + 2 * SUBLANES, cch), F32),
            pltpu.VMEM((blk, cch), F32),
            pltpu.VMEM((blk, LANES), F32),
            pltpu.VMEM((blk, inner), F32),
            pltpu.VMEM((SSD_GROUPS, SSD_STATE, inner // SSD_GROUPS), F32),
        ],
        compiler_params=pltpu.CompilerParams(
            dimension_semantics=("arbitrary", "arbitrary"), vmem_limit_bytes=VMEM_LIMIT),
        name="ssd_layer",
    )(x, sh, sc, gt, norm_g.reshape(1, d), w_z, w_xbc, w_dt, conv_w, conv_b.reshape(1, cch),
      dtb, alog, d_full, gnorm.reshape(1, inner), w_out.astype(BF16), pairsel)


def _route(logits_t, rb_ref):
    scores = _sigmoid(logits_t)
    sc = [scores[e:e + 1, :] for e in range(N_EXPERTS)]
    bi = [sc[e] + rb_ref[e:e + 1, :] for e in range(N_EXPERTS)]
    gscore = []
    for g in range(N_GROUPS):
        a, b, c, d = bi[4 * g:4 * g + 4]
        hi1, lo1 = jnp.maximum(a, b), jnp.minimum(a, b)
        hi2, lo2 = jnp.maximum(c, d), jnp.minimum(c, d)
        gscore.append(jnp.maximum(hi1, hi2) + jnp.maximum(jnp.minimum(hi1, hi2), jnp.maximum(lo1, lo2)))
    gates = []
    flags = []
    for g in range(N_GROUPS):
        gsel = None
        for o in range(N_GROUPS):
            if o == g:
                continue
            win = (gscore[g] >= gscore[o]) if o > g else (gscore[g] > gscore[o])
            gsel = win if gsel is None else jnp.logical_and(gsel, win)
        sel = []
        for i in range(EXPERTS_PER_GROUP):
            e = 4 * g + i
            rank = jnp.zeros_like(bi[e])
            for j in range(EXPERTS_PER_GROUP):
                if j == i:
                    continue
                o = 4 * g + j
                beats = (bi[o] >= bi[e]) if j < i else (bi[o] > bi[e])
                rank = rank + jnp.where(beats, 1.0, 0.0)
            chosen = jnp.logical_and(gsel, rank < 2.0)
            flags.append(jnp.where(chosen, 1.0, 0.0))
            sel.append(jnp.where(chosen, sc[e], 0.0))
        denom = sel[0] + sel[1] + sel[2] + sel[3]
        denom = jnp.where(gsel, denom, 1.0)
        gates.extend([s_ / denom for s_ in sel])
    return gates, flags


def _router_kernel(x_ref, sh_ref, sc_ref, nm_ref, rwt_ref, rb_ref, upper_ref,
                   h_ref, slot_ref, col_ref, cnt_ref):
    tm = x_ref.shape[0]
    hb = _ada_norm(x_ref[...], nm_ref[...], sh_ref[0], sc_ref[0]).astype(BF16)
    h_ref[...] = hb
    gates, flags = _route(_dot_nt(rwt_ref[...], hb), rb_ref)
    big = float(N_EXPERTS)
    e_lo = functools.reduce(jnp.minimum, [jnp.where(flags[e] > 0.0, float(e), big) for e in range(N_EXPERTS)])
    e_hi = functools.reduce(jnp.maximum, [jnp.where(flags[e] > 0.0, float(e), -1.0) for e in range(N_EXPERTS)])
    flag_mat = jnp.concatenate(flags, axis=0)
    prefix = _dot(flag_mat.astype(BF16), upper_ref[...])
    counts = jnp.sum(flag_mat, axis=1, keepdims=True)
    slot_of = []
    before = jnp.zeros((1, 1), F32)
    for e in range(N_EXPERTS):
        slot_of.append(prefix[e:e + 1, :] + before)
        before = before + jnp.floor((counts[e:e + 1, :] + (SUBLANES - 1)) * (1.0 / SUBLANES)) * SUBLANES

    def pick(eid, rows):
        return functools.reduce(
            jnp.add, [jnp.where(eid == float(e), rows[e], 0.0) for e in range(N_EXPERTS)])

    zero = jnp.zeros((1, tm), F32)
    s_lo, s_hi = pick(e_lo, slot_of), pick(e_hi, slot_of)
    slot_ref[0] = jnp.concatenate([s_lo, s_hi] + [zero] * (SUBLANES - 2), axis=0).astype(jnp.int32)
    rows = jnp.concatenate([s_lo, s_hi, pick(e_lo, gates), pick(e_hi, gates)] + [zero] * (N_EXPERTS - 4), axis=0)
    col_ref[...] = rows.T
    cnt_ref[0] = jnp.broadcast_to(counts, (N_EXPERTS, LANES))


def _router_call(x2d, sh, sc, norm_g, router_w, router_b, *, tm, tiles_per_batch):
    t, d = x2d.shape
    nt = t // tm
    row3 = pl.BlockSpec((1, 1, d), lambda i: (i // tiles_per_batch, 0, 0))
    rb = jnp.broadcast_to(router_b.reshape(N_EXPERTS, 1), (N_EXPERTS, tm))
    upper = (jnp.arange(tm)[:, None] < jnp.arange(tm)[None, :]).astype(BF16)
    return pl.pallas_call(
        _router_kernel,
        grid=(nt,),
        in_specs=[
            pl.BlockSpec((tm, d), lambda i: (i, 0)), row3, row3,
            _const_spec((1, d)),
            _const_spec((N_EXPERTS, d)),
            _const_spec((N_EXPERTS, tm)),
            _const_spec((tm, tm)),
        ],
        out_specs=[
            pl.BlockSpec((tm, d), lambda i: (i, 0)),
            pl.BlockSpec((1, SUBLANES, tm), lambda i: (i, 0, 0)),
            pl.BlockSpec((tm, N_EXPERTS), lambda i: (i, 0)),
            pl.BlockSpec((1, N_EXPERTS, LANES), lambda i: (i, 0, 0)),
        ],
        out_shape=[
            jax.ShapeDtypeStruct((t, d), BF16),
            jax.ShapeDtypeStruct((nt, SUBLANES, tm), jnp.int32),
            jax.ShapeDtypeStruct((t, N_EXPERTS), F32),
            jax.ShapeDtypeStruct((nt, N_EXPERTS, LANES), F32),
        ],
        compiler_params=pltpu.CompilerParams(dimension_semantics=("arbitrary",), vmem_limit_bytes=VMEM_LIMIT),
        name="moe_router",
    )(x2d, sh, sc, norm_g.reshape(1, d), router_w.T.astype(BF16), rb, upper)


def _pack_pair(a, b):
    ua = lax.bitcast_convert_type(a, jnp.uint32)
    ub = lax.bitcast_convert_type(b, jnp.uint32)
    return (ua >> 16) | (ub & jnp.uint32(0xFFFF0000))


def _unpack_pair(w):
    a = lax.bitcast_convert_type(w << 16, F32)
    b = lax.bitcast_convert_type(w & jnp.uint32(0xFFFF0000), F32)
    return a.astype(BF16), b.astype(BF16)


def _bf16_exact(x):
    return x.astype(BF16).astype(F32)


def _slot_rows(tm):
    return 2 * tm + N_EXPERTS * SUBLANES


def _run_copies(n_rows, max_rows, make_copy, op):
    for b in reversed(range(SUBLANES.bit_length() - 1, max_rows.bit_length())):
        size = 1 << b
        offset = pl.multiple_of((n_rows >> (b + 1)) << (b + 1), SUBLANES)

        @pl.when((n_rows & size) != 0)
        def _(size=size, offset=offset):
            op(make_copy(offset, size))


def _tile_runs_start(tile, tm, len_ref, lbase_ref, toff_ref, local_ref, global_ref, sem, to_global):
    for e in range(N_EXPERTS):
        idx = tile * N_EXPERTS + e
        lo = lbase_ref[idx]
        go = toff_ref[idx]

        def make_copy(offset, size, lo=lo, go=go):
            loc = local_ref.at[pl.ds(pl.multiple_of(lo + offset, SUBLANES), size)]
            glb = global_ref.at[pl.ds(pl.multiple_of(go + offset, SUBLANES), size)]
            return pltpu.make_async_copy(loc, glb, sem) if to_global else pltpu.make_async_copy(glb, loc, sem)

        _run_copies(len_ref[idx], tm, make_copy, _start)


def _tile_runs_wait(tile, tm, rows_ref, local_ref, global_ref, sem, to_global):
    def make_copy(offset, size):
        del offset
        loc = local_ref.at[pl.ds(0, size)]
        glb = global_ref.at[pl.ds(0, size)]
        return pltpu.make_async_copy(loc, glb, sem) if to_global else pltpu.make_async_copy(glb, loc, sem)

    _run_copies(rows_ref[tile], _slot_rows(tm), make_copy, _wait)


def _start(copy):
    copy.start()


def _wait(copy):
    copy.wait()


def _dispatch_kernel(cnt_ref, lbase_ref, toff_ref, rows_ref, padstart_ref, padlen_ref, nact_ref,
                     slot_ref, h_ref, xs_ref, buf_ref, zero_ref, sem, zsem, *, rt):
    i = pl.program_id(0)
    last = pl.num_programs(0) - 1
    cur = i % 2
    tm = h_ref.shape[0]
    half = h_ref.shape[1] // 2

    nslot = _slot_rows(tm)

    def wait_runs(tile, which):
        _tile_runs_wait(tile, tm, rows_ref, buf_ref.at[which], xs_ref, sem.at[which], True)

    @pl.when(i > 0)
    def _():
        wait_runs(i - 1, 1 - cur)

    sl = slot_ref[0]
    srow = lax.broadcasted_iota(jnp.int32, (nslot, tm), 0)
    perm = jnp.where(srow == sl[0:1, :], 1.0, jnp.where(srow == sl[1:2, :], 1.0, 0.0)).astype(BF16)
    rows = _dot(perm, h_ref[...])
    buf_ref[cur] = _pack_pair(rows[:, :half], rows[:, half:])
    _tile_runs_start(i, tm, cnt_ref, lbase_ref, toff_ref, buf_ref.at[cur], xs_ref, sem.at[cur], True)

    @pl.when(i == last)
    def _():
        wait_runs(i, cur)
        zero_ref[...] = jnp.zeros_like(zero_ref)
        for op in (_start, _wait):
            for e in range(N_EXPERTS):
                def make_copy(offset, size, e=e):
                    dst = xs_ref.at[pl.ds(pl.multiple_of(padstart_ref[e] + offset, SUBLANES), size)]
                    return pltpu.make_async_copy(zero_ref.at[pl.ds(0, size)], dst, zsem)
                _run_copies(padlen_ref[e], rt - 1, make_copy, op)

        def zero_tile(j, carry):
            copy = pltpu.make_async_copy(zero_ref, xs_ref.at[pl.ds(pl.multiple_of(j * rt, rt), rt)], zsem)
            copy.start()
            copy.wait()
            return carry

        lax.fori_loop(nact_ref[0], xs_ref.shape[0] // rt, zero_tile, 0)


def _dispatch_call(seg, slot_rows, h2d, n_rows, *, rt):
    t, d = h2d.shape
    nt, _, tm = slot_rows.shape
    grid_spec = pltpu.PrefetchScalarGridSpec(
        num_scalar_prefetch=7,
        grid=(nt,),
        in_specs=[
            pl.BlockSpec((1, SUBLANES, tm), lambda i, *_: (i, 0, 0)),
            pl.BlockSpec((tm, d), lambda i, *_: (i, 0)),
        ],
        out_specs=pl.BlockSpec(memory_space=pl.ANY),
        scratch_shapes=[
            pltpu.VMEM((2, _slot_rows(tm), d // 2), jnp.uint32),
            pltpu.VMEM((rt, d // 2), jnp.uint32),
            pltpu.SemaphoreType.DMA((2,)),
            pltpu.SemaphoreType.DMA,
        ],
    )
    return pl.pallas_call(
        functools.partial(_dispatch_kernel, rt=rt),
        grid_spec=grid_spec,
        out_shape=jax.ShapeDtypeStruct((n_rows, d // 2), jnp.uint32),
        compiler_params=pltpu.CompilerParams(dimension_semantics=("arbitrary",), vmem_limit_bytes=VMEM_LIMIT),
        name="moe_dispatch",
    )(seg["cnt"], seg["lbase"], seg["toff"], seg["rows"], seg["padstart"], seg["padlen"], seg["nact"],
      slot_rows, h2d)


def _expert_kernel(te_ref, nact_ref, xs_ref, wg_ref, wu_ref, wd_ref, o_ref, wgb_ref, wub_ref, wdb_ref):
    j = pl.program_id(0)
    active = j < nact_ref[0]
    half = xs_ref.shape[1]

    @pl.when(jnp.logical_and(active, jnp.logical_or(j == 0, te_ref[j] != te_ref[jnp.maximum(j - 1, 0)])))
    def _():
        wgb_ref[...] = wg_ref[0, 0].astype(BF16)
        wub_ref[...] = wu_ref[0, 0].astype(BF16)
        wdb_ref[...] = wd_ref[0, 0].astype(BF16)

    @pl.when(active)
    def _():
        xa, xb = _unpack_pair(xs_ref[...])
        gate = _dot(xa, wgb_ref[0:half, :]) + _dot(xb, wgb_ref[half:, :])
        up = _dot(xa, wub_ref[0:half, :]) + _dot(xb, wub_ref[half:, :])
        he = (_silu(gate) * up).astype(BF16)
        y = _dot(he, wdb_ref[...])
        o_ref[...] = _pack_pair(_bf16_exact(y[:, :half]), _bf16_exact(y[:, half:]))

    @pl.when(jnp.logical_not(active))
    def _():
        o_ref[...] = jnp.zeros_like(o_ref)


def _expert_call(tile_expert, nact, xs, w_gate, w_up, w_down, *, layer, rt):
    n_rows, half = xs.shape
    _, _, d, de = w_gate.shape

    def row_map(j, te, na):
        return (jnp.minimum(j, na[0] - 1), 0)

    grid_spec = pltpu.PrefetchScalarGridSpec(
        num_scalar_prefetch=2,
        grid=(n_rows // rt,),
        in_specs=[
            pl.BlockSpec((rt, half), row_map),
            pl.BlockSpec((1, 1, d, de), lambda j, te, na: (layer, te[j], 0, 0)),
            pl.BlockSpec((1, 1, d, de), lambda j, te, na: (layer, te[j], 0, 0)),
            pl.BlockSpec((1, 1, de, d), lambda j, te, na: (layer, te[j], 0, 0)),
        ],
        out_specs=pl.BlockSpec((rt, half), lambda j, te, na: (j, 0)),
        scratch_shapes=[pltpu.VMEM((d, de), BF16), pltpu.VMEM((d, de), BF16), pltpu.VMEM((de, d), BF16)],
    )
    return pl.pallas_call(
        _expert_kernel,
        grid_spec=grid_spec,
        out_shape=jax.ShapeDtypeStruct((n_rows, half), jnp.uint32),
        compiler_params=pltpu.CompilerParams(dimension_semantics=("arbitrary",), vmem_limit_bytes=VMEM_LIMIT),
        name="moe_experts",
    )(tile_expert, nact, xs, w_gate, w_up, w_down)


def _combine_kernel(cnt_ref, lbase_ref, toff_ref, rows_ref, x_ref, gt_ref, col_ref, fin_ref, ys_ref, o_ref,
                    buf_ref, sem, *, final):
    i = pl.program_id(0)
    n = pl.num_programs(0)
    cur = i % 2
    tm = x_ref.shape[0]

    nslot = _slot_rows(tm)

    def start_runs(tile, which):
        _tile_runs_start(tile, tm, cnt_ref, lbase_ref, toff_ref, buf_ref.at[which], ys_ref, sem.at[which], False)

    @pl.when(i == 0)
    def _():
        buf_ref[...] = jnp.zeros_like(buf_ref)
        start_runs(i, cur)

    @pl.when(i + 1 < n)
    def _():
        start_runs(i + 1, 1 - cur)

    _tile_runs_wait(i, tm, rows_ref, buf_ref.at[cur], ys_ref, sem.at[cur], False)
    ya, yb = _unpack_pair(buf_ref[cur])
    col = col_ref[...]
    lane = lax.broadcasted_iota(jnp.int32, (tm, nslot), 1).astype(F32)
    unsort = jnp.where(lane == col[:, 0:1], col[:, 2:3],
                       jnp.where(lane == col[:, 1:2], col[:, 3:4], 0.0)).astype(BF16)
    moe = jnp.concatenate([_dot(unsort, ya), _dot(unsort, yb)], axis=1)
    y = x_ref[...] + gt_ref[0] * moe
    if final:
        y = _rms(y) * fin_ref[...]
    o_ref[...] = y


def _combine_call(seg, x2d, gt, cols, norm_final, ys, *, tm, tiles_per_batch, final):
    t, d = x2d.shape
    nt = t // tm
    grid_spec = pltpu.PrefetchScalarGridSpec(
        num_scalar_prefetch=4,
        grid=(nt,),
        in_specs=[
            pl.BlockSpec((tm, d), lambda i, *_: (i, 0)),
            pl.BlockSpec((1, 1, d), lambda i, *_: (i // tiles_per_batch, 0, 0)),
            pl.BlockSpec((tm, N_EXPERTS), lambda i, *_: (i, 0)),
            pl.BlockSpec((1, d), lambda i, *_: (0, 0)),
            pl.BlockSpec(memory_space=pl.ANY),
        ],
        out_specs=pl.BlockSpec((tm, d), lambda i, *_: (i, 0)),
        scratch_shapes=[pltpu.VMEM((2, _slot_rows(tm), d // 2), jnp.uint32), pltpu.SemaphoreType.DMA((2,))],
    )
    return pl.pallas_call(
        functools.partial(_combine_kernel, final=final),
        grid_spec=grid_spec,
        out_shape=jax.ShapeDtypeStruct((t, d), F32),
        compiler_params=pltpu.CompilerParams(dimension_semantics=("arbitrary",), vmem_limit_bytes=VMEM_LIMIT),
        name="moe_combine",
    )(seg["cnt"], seg["lbase"], seg["toff"], seg["rows"], x2d, gt, cols, norm_final.reshape(1, d), ys)


def _moe_layer(x, sh, sc, gt, norm_g, router_w, router_b, w_gate, w_up, w_down, norm_final, *,
               layer, tm, rt, final):
    bsz, s, d = x.shape
    t = bsz * s
    x2d = x.reshape(t, d)
    tiles_per_batch = s // tm
    h2d, slot_rows, cols, cnt = _router_call(x2d, sh, sc, norm_g, router_w, router_b, tm=tm,
                                             tiles_per_batch=tiles_per_batch)
    cnt = cnt[:, :, 0].astype(jnp.int32)
    run = ((cnt + SUBLANES - 1) // SUBLANES) * SUBLANES
    total = jnp.sum(run, axis=0)
    padded = ((total + rt - 1) // rt) * rt
    ends = jnp.cumsum(padded)
    base = ends - padded
    nt = t // tm
    n_rows = -(-(2 * t + nt * N_EXPERTS * SUBLANES) // rt) * rt + N_EXPERTS * rt
    starts = jnp.arange(n_rows // rt, dtype=jnp.int32) * rt
    tile_expert = jnp.minimum(jnp.sum(starts[:, None] >= ends[None, :], axis=1), N_EXPERTS - 1).astype(jnp.int32)
    nact = (ends[-1:] // rt).astype(jnp.int32)
    seg = {
        "cnt": run.reshape(-1),
        "lbase": (jnp.cumsum(run, axis=1) - run).reshape(-1),
        "toff": (base[None, :] + jnp.cumsum(run, axis=0) - run).reshape(-1),
        "rows": jnp.sum(run, axis=1),
        "padstart": base + total,
        "padlen": padded - total,
        "nact": nact,
    }
    seg = {k: v.astype(jnp.int32) for k, v in seg.items()}
    xs = _dispatch_call(seg, slot_rows, h2d, n_rows, rt=rt)
    ys = _expert_call(tile_expert, nact, xs, w_gate, w_up, w_down, layer=layer, rt=rt)
    out = _combine_call(seg, x2d, gt, cols, norm_final, ys, tm=tm, tiles_per_batch=tiles_per_batch, final=final)
    return out.reshape(bsz, s, d)


def kernel(x, c, ada_w, ada_b, norm_mix, norm_ffn, norm_final, gla_w_in, gla_w_gate2, gla_b_gate2, gla_norm,
           gla_w_out, ssd_w_in, ssd_conv_w, ssd_conv_b, ssd_dt_bias, ssd_a_log, ssd_d, ssd_norm, ssd_w_out,
           router_w, router_b, moe_w_gate, moe_w_up, moe_w_down):
    depth = ada_w.shape[0]
    bsz, s, d = x.shape
    mod = _ada_call(c, ada_w, ada_b).reshape(depth, bsz, 6, 1, d)
    n_mixers = 2
    for i in range(depth):
        sh1, sc1, g1, sh2, sc2, g2 = (mod[i, :, t] for t in range(6))
        j = i // n_mixers
        if i % n_mixers == 0:
            x = _gla_call(x, sh1, sc1, g1, norm_mix[i], gla_w_in[j], gla_w_gate2[j], gla_b_gate2[j],
                          gla_norm[j], gla_w_out[j], blk=min(1024, s))
        else:
            x = _ssd_call(x, sh1, sc1, g1, norm_mix[i], ssd_w_in[j], ssd_conv_w[j], ssd_conv_b[j],
                          ssd_dt_bias[j], ssd_a_log[j], ssd_d[j], ssd_norm[j], ssd_w_out[j], blk=min(512, s))
        x = _moe_layer(x, sh2, sc2, g2, norm_ffn[i], router_w, router_b, moe_w_gate, moe_w_up, moe_w_down,
                       norm_final, layer=i, tm=min(256, s), rt=512, final=(i == depth - 1))
    return x
```
